```python
import math
import jax, jax.numpy as jnp
from jax import lax
import numpy as np

D_MODEL = 2048
BATCH = 2
SEQ = 4096
DEPTH = 1

GRID_W = 64
CTX_LEN = 256
DIFF_HEAD_DIM = 64
DIFF_WIDTH = D_MODEL // 2
N_DIFF_HEADS = DIFF_WIDTH // (2 * DIFF_HEAD_DIM)
FOURIER_WIDTH = D_MODEL // 2
N_FOURIER_GROUPS = 4
FOURIER_GROUP_DIM = FOURIER_WIDTH // N_FOURIER_GROUPS
D_FF = 4 * D_MODEL
N_MOD = 6
Q_BLOCK = 128
ROPE_BASE = 10000.0
EPS = 1e-6
IN_COLS = 3 * DIFF_WIDTH + FOURIER_WIDTH + 2 * D_MODEL

kernel_name = "hybrid_diffattn_fourier_dit_block"


def rmsnorm(x, g):
    xf = x.astype(jnp.float32)
    y = xf * lax.rsqrt(jnp.mean(xf * xf, axis=-1, keepdims=True) + EPS)
    return (y * g.astype(jnp.float32)).astype(x.dtype)


def modulate(h, shift, scale):
    return h * (1.0 + scale) + shift


def axial_tables(seq_len):
    rows_n = seq_len // GRID_W
    row = jnp.repeat(jnp.arange(rows_n), GRID_W)
    col = jnp.tile(jnp.arange(GRID_W), rows_n)
    half = DIFF_HEAD_DIM // 2
    inv_freq = ROPE_BASE ** (-jnp.arange(0, half, 2, dtype=jnp.float32) / half)

    def cs(pos):
        ang = pos.astype(jnp.float32)[:, None] * inv_freq[None, :]
        ang = jnp.concatenate([ang, ang], axis=-1)
        return jnp.cos(ang), jnp.sin(ang)

    cr, sr = cs(row)
    cc, sc = cs(col)
    return cr, sr, cc, sc


def _rotate(x, cos, sin):
    cos = cos[None, :, None, None, :]
    sin = sin[None, :, None, None, :]
    x1, x2 = jnp.split(x, 2, axis=-1)
    return x * cos + jnp.concatenate([-x2, x1], axis=-1) * sin


def axial_rope(x, tabs):
    cr, sr, cc, sc = tabs
    half = DIFF_HEAD_DIM // 2
    xf = x.astype(jnp.float32)
    y = jnp.concatenate([_rotate(xf[..., :half], cr, sr), _rotate(xf[..., half:], cc, sc)], axis=-1)
    return y.astype(x.dtype)


def diff_attention(q, k, v, lam):
    B, Lq, H, _, Dh = q.shape
    nblk = Lq // Q_BLOCK
    scale = Dh ** -0.5
    kf = k.astype(jnp.float32)
    vf = v.astype(jnp.float32)
    qb = q.astype(jnp.float32).reshape(B, nblk, Q_BLOCK, H, 2, Dh).transpose(1, 0, 2, 3, 4, 5)

    def block(qblk):
        s = jnp.einsum('bqhcd,bkhcd->bhcqk', qblk, kf) * scale
        p = jax.nn.softmax(s, axis=-1)
        a = p[:, :, 0] - lam * p[:, :, 1]
        return jnp.einsum('bhqk,bkhe->bqhe', a, vf)

    o = lax.map(block, qb)
    return o.transpose(1, 0, 2, 3, 4).reshape(B, Lq, H, 2 * Dh)


def fourier_mix(f):
    B, L, _ = f.shape
    u = f.reshape(B, L, N_FOURIER_GROUPS, FOURIER_GROUP_DIM).astype(jnp.float32)
    y = jnp.fft.fft2(u, axes=(1, 3), norm="ortho").real
    return y.reshape(B, L, FOURIER_WIDTH).astype(f.dtype)


def split_proj(p):
    B, L, _ = p.shape
    o = 0
    q = p[..., o:o + DIFF_WIDTH].reshape(B, L, N_DIFF_HEADS, 2, DIFF_HEAD_DIM); o += DIFF_WIDTH
    k = p[..., o:o + DIFF_WIDTH].reshape(B, L, N_DIFF_HEADS, 2, DIFF_HEAD_DIM); o += DIFF_WIDTH
    v = p[..., o:o + DIFF_WIDTH].reshape(B, L, N_DIFF_HEADS, 2 * DIFF_HEAD_DIM); o += DIFF_WIDTH
    f = p[..., o:o + FOURIER_WIDTH]; o += FOURIER_WIDTH
    ga = p[..., o:o + D_MODEL]; o += D_MODEL
    gf = p[..., o:o + D_MODEL]
    return q, k, v, f, ga, gf


def merge_branches(attn_o, f, ga, gf, g_subln, lam_init, w_attn_br, w_four_br, w_out):
    B, L = attn_o.shape[:2]
    a = (rmsnorm(attn_o, g_subln) * (1.0 - lam_init)).reshape(B, L, DIFF_WIDTH)
    ya = a @ w_attn_br
    yf = fourier_mix(f) @ w_four_br
    merged = jax.nn.sigmoid(ga) * ya + jax.nn.sigmoid(gf) * yf
    return merged @ w_out


def sq_relu_mlp(h, w1, w2):
    z = jax.nn.relu(h @ w1)
    return (z * z) @ w2


def setup_inputs(seed: int = 0) -> dict:
    key = jax.random.key(seed)
    ks = jax.random.split(key, 24)
    nrm = lambda k, shape, s: jax.random.normal(k, shape, jnp.float32) * s
    gain = lambda k, shape: 1.0 + 0.02 * jax.random.normal(k, shape, jnp.float32)
    return {
        "x": nrm(ks[0], (BATCH, SEQ, D_MODEL), 1.0),
        "c": nrm(ks[1], (BATCH, D_MODEL), 1.0),
        "ctx": nrm(ks[2], (BATCH, CTX_LEN, D_MODEL), 1.0),
        "c_ctx": nrm(ks[3], (D_MODEL,), 1.0),
        "w_ada": nrm(ks[4], (DEPTH, D_MODEL, N_MOD * D_MODEL), 0.5 * D_MODEL ** -0.5),
        "b_ada": nrm(ks[5], (DEPTH, N_MOD * D_MODEL), 0.02),
        "g_norm1": gain(ks[6], (DEPTH, D_MODEL)),
        "w_in": nrm(ks[7], (DEPTH, D_MODEL, IN_COLS), D_MODEL ** -0.5),
        "lam_q1": nrm(ks[8], (DEPTH, DIFF_HEAD_DIM), 0.1),
        "lam_k1": nrm(ks[9], (DEPTH, DIFF_HEAD_DIM), 0.1),
        "lam_q2": nrm(ks[10], (DEPTH, DIFF_HEAD_DIM), 0.1),
        "lam_k2": nrm(ks[11], (DEPTH, DIFF_HEAD_DIM), 0.1),
        "g_subln": gain(ks[12], (DEPTH, 2 * DIFF_HEAD_DIM)),
        "w_attn_br": nrm(ks[13], (DEPTH, DIFF_WIDTH, D_MODEL), DIFF_WIDTH ** -0.5),
        "w_four_br": nrm(ks[14], (DEPTH, FOURIER_WIDTH, D_MODEL), FOURIER_WIDTH ** -0.5),
        "w_out": nrm(ks[15], (DEPTH, D_MODEL, D_MODEL), D_MODEL ** -0.5),
        "g_norm2": gain(ks[16], (DEPTH, D_MODEL)),
        "w_mlp_in": nrm(ks[17], (DEPTH, D_MODEL, D_FF), D_MODEL ** -0.5),
        "w_mlp_out": nrm(ks[18], (DEPTH, D_FF, D_MODEL), D_FF ** -0.5),
        "g_final": gain(ks[19], (D_MODEL,)),
    }


def reference(x, c, ctx, c_ctx, w_ada, b_ada, g_norm1, w_in, lam_q1, lam_k1, lam_q2, lam_k2,
              g_subln, w_attn_br, w_four_br, w_out, g_norm2, w_mlp_in, w_mlp_out, g_final):
    tabs = axial_tables(x.shape[1])
    xl = x
    xc = ctx
    for l in range(DEPTH):
        lam_init = 0.8 - 0.6 * math.exp(-0.3 * l)
        last = l == DEPTH - 1
        mod = (jax.nn.silu(c) @ w_ada[l] + b_ada[l])[:, None, :]
        sh1, sc1, gt1, sh2, sc2, gt2 = jnp.split(mod, N_MOD, axis=-1)
        modc = jax.nn.silu(c_ctx) @ w_ada[l] + b_ada[l]
        sh1c, sc1c, gt1c, sh2c, sc2c, gt2c = jnp.split(modc, N_MOD, axis=-1)
        lam = (jnp.exp(jnp.sum(lam_q1[l].astype(jnp.float32) * lam_k1[l].astype(jnp.float32)))
               - jnp.exp(jnp.sum(lam_q2[l].astype(jnp.float32) * lam_k2[l].astype(jnp.float32)))
               + lam_init)

        hl = modulate(rmsnorm(xl, g_norm1[l]), sh1, sc1)
        hc = modulate(rmsnorm(xc, g_norm1[l]), sh1c, sc1c)
        ql, kl, vl, fl, gal, gfl = split_proj(hl @ w_in[l])
        qc, kc, vc, fc, gac, gfc = split_proj(hc @ w_in[l])
        k_all = jnp.concatenate([axial_rope(kl, tabs), kc.astype(kl.dtype)], axis=1)
        v_all = jnp.concatenate([vl, vc.astype(vl.dtype)], axis=1)
        ol = diff_attention(axial_rope(ql, tabs), k_all, v_all, lam)
        xl_mid = xl + gt1 * merge_branches(ol, fl, gal, gfl, g_subln[l], lam_init,
                                           w_attn_br[l], w_four_br[l], w_out[l])
        if not last:
            oc = diff_attention(qc, kc, vc, lam)
            xc = xc + gt1c * merge_branches(oc, fc, gac, gfc, g_subln[l], lam_init,
                                            w_attn_br[l], w_four_br[l], w_out[l])
            xc = xc + gt2c * sq_relu_mlp(modulate(rmsnorm(xc, g_norm2[l]), sh2c, sc2c),
                                         w_mlp_in[l], w_mlp_out[l])
        xl = xl_mid + gt2 * sq_relu_mlp(modulate(rmsnorm(xl_mid, g_norm2[l]), sh2, sc2),
                                        w_mlp_in[l], w_mlp_out[l])
    return rmsnorm(xl, g_final)
```

```python
import functools
import math

import jax
import jax.numpy as jnp
import numpy as np
from jax import lax
from jax.experimental import pallas as pl
from jax.experimental.pallas import tpu as pltpu

GRID_W = 64
HEAD_DIM = 64
HEAD_COLS = 2 * HEAD_DIM
N_FOURIER_GROUPS = 4
N_MOD = 6
ROPE_BASE = 10000.0
EPS = 1e-6
LAM_INIT = 0.8 - 0.6 * math.exp(-0.3 * 0)

V7X_VMEM_LIMIT_BYTES = 56 * 1024 * 1024
BF16 = jnp.bfloat16
F32 = jnp.float32


def _params(semantics):
    return pltpu.CompilerParams(dimension_semantics=semantics,
                                vmem_limit_bytes=V7X_VMEM_LIMIT_BYTES)


def _dot(a, b):
    return jnp.dot(a, b, preferred_element_type=F32)


def _dot_nt(a, b):
    return lax.dot_general(a, b, (((1,), (1,)), ((), ())), preferred_element_type=F32)


def _resident(shape):
    zeros = (0,) * len(shape)
    return pl.BlockSpec(shape, lambda *_: zeros, pipeline_mode=pl.Buffered(1))


def _ada_kernel(c_ref, w_ref, b_ref, o_ref):
    cv = c_ref[...]
    o_ref[...] = _dot(cv * jax.nn.sigmoid(cv), w_ref[...]) + b_ref[...]


def ada_mod(cvec, w_ada, b_ada, *, tn=1024):
    rows, d = cvec.shape
    n = w_ada.shape[1]
    return pl.pallas_call(
        _ada_kernel,
        grid=(n // tn,),
        in_specs=[pl.BlockSpec((rows, d), lambda j: (0, 0)),
                  pl.BlockSpec((d, tn), lambda j: (0, j)),
                  pl.BlockSpec((1, tn), lambda j: (0, j))],
        out_specs=pl.BlockSpec((rows, tn), lambda j: (0, j)),
        out_shape=jax.ShapeDtypeStruct((rows, n), F32),
        compiler_params=_params(("arbitrary",)),
        name="ada_mod",
    )(cvec, w_ada, b_ada.reshape(1, n))


def _rms(x):
    return x * lax.rsqrt(jnp.mean(x * x, axis=-1, keepdims=True) + EPS)


def _norm_mod(x, g, shift, scale):
    return _rms(x) * g * (1.0 + scale) + shift


def _in_proj_kernel(x_ref, g_ref, sh_ref, sc_ref, w_ref, tab_ref, o_ref, h_ref, *,
                    rope_tiles, gate_tile0):
    j = pl.program_id(1)

    @pl.when(j == 0)
    def _():
        h = _norm_mod(x_ref[...], g_ref[...], sh_ref[0], sc_ref[0])
        h_ref[...] = h.astype(BF16)

    acc = _dot(h_ref[...], w_ref[...])
    tn = acc.shape[1]

    if rope_tiles:
        @pl.when(j < rope_tiles)
        def _():
            tab = tab_ref[0]
            cos = tab[:, :HEAD_COLS]
            sin_lo = tab[:, HEAD_COLS:2 * HEAD_COLS]
            sin_hi = tab[:, 2 * HEAD_COLS:]
            half = HEAD_DIM // 4
            for c in range(tn // HEAD_COLS):
                xs = acc[:, c * HEAD_COLS:(c + 1) * HEAD_COLS]
                up = pltpu.roll(xs, HEAD_COLS - half, 1)
                dn = pltpu.roll(xs, half, 1)
                y = xs * cos + up * sin_lo + dn * sin_hi
                o_ref[:, c * HEAD_COLS:(c + 1) * HEAD_COLS] = y.astype(o_ref.dtype)

        @pl.when((j >= rope_tiles) & (j < gate_tile0))
        def _():
            o_ref[...] = acc.astype(o_ref.dtype)

        @pl.when(j >= gate_tile0)
        def _():
            o_ref[...] = jax.nn.sigmoid(acc).astype(o_ref.dtype)
    else:
        o_ref[...] = acc.astype(o_ref.dtype)


def in_proj(x2d, g, mod3, mod_row0, w_bf, tabs, *, rows_per_mod, col_tile0, n_cols,
            rope_tiles, gate_tile0, tm, tn):
    m, d = x2d.shape
    tiles_per_mod = rows_per_mod // tm
    q_tiles = rope_tiles // 2 if rope_tiles else 1
    ltiles = tabs.shape[1] // tm if rope_tiles else 1

    def mod_map(which):
        return lambda i, j: ((mod_row0 + i // tiles_per_mod) * N_MOD + which, 0, 0)

    if rope_tiles:
        tab_spec = pl.BlockSpec((1, tm, 3 * HEAD_COLS),
                                lambda i, j: (jnp.minimum(j // q_tiles, 1), i % ltiles, 0))
    else:
        tab_spec = pl.BlockSpec((1, 8, 3 * HEAD_COLS), lambda i, j: (0, 0, 0))

    kern = functools.partial(_in_proj_kernel, rope_tiles=rope_tiles, gate_tile0=gate_tile0)
    return pl.pallas_call(
        kern,
        grid=(m // tm, n_cols // tn),
        in_specs=[pl.BlockSpec((tm, d), lambda i, j: (i, 0)),
                  pl.BlockSpec((1, d), lambda i, j: (0, 0)),
                  pl.BlockSpec((1, 1, d), mod_map(0)),
                  pl.BlockSpec((1, 1, d), mod_map(1)),
                  pl.BlockSpec((d, tn), lambda i, j: (0, j + col_tile0)),
                  tab_spec],
        out_specs=pl.BlockSpec((tm, tn), lambda i, j: (i, j)),
        out_shape=jax.ShapeDtypeStruct((m, n_cols), BF16),
        scratch_shapes=[pltpu.VMEM((tm, d), BF16)],
        compiler_params=_params(("arbitrary", "arbitrary")),
        name="in_proj",
    )(x2d, g.reshape(1, d), mod3, mod3, w_bf, tabs)


def rope_tables(seq_len):
    pos = np.arange(seq_len)
    row = pos // GRID_W
    col = pos % GRID_W
    half = HEAD_DIM // 2
    inv_freq = ROPE_BASE ** (-np.arange(0, half, 2, dtype=np.float64) / half)

    def cs(p):
        ang = p[:, None].astype(np.float64) * inv_freq[None, :]
        ang = np.concatenate([ang, ang], axis=-1)
        return np.cos(ang), np.sin(ang)

    cr, sr = cs(row)
    cc, sc = cs(col)
    cos64 = np.concatenate([cr, cc], axis=-1)
    sin64 = np.concatenate([sr, sc], axis=-1)
    lower = (np.arange(HEAD_DIM) % half) < (half // 2)
    sin_lo64 = np.where(lower[None, :], -sin64, 0.0)
    sin_hi64 = np.where(lower[None, :], 0.0, sin64)
    tab = np.concatenate([np.tile(cos64, (1, 2)), np.tile(sin_lo64, (1, 2)),
                          np.tile(sin_hi64, (1, 2))], axis=-1)
    q_scale = HEAD_DIM ** -0.5 * math.log2(math.e)
    return jnp.asarray(np.stack([tab * q_scale, tab]), dtype=F32)


def _attn_kernel(lq1_ref, lk1_ref, lq2_ref, lk2_ref, gs_ref, q_ref, k_ref, v_ref, kc_ref, vc_ref,
                 o_ref):
    lam = (jnp.exp(jnp.sum(lq1_ref[...] * lk1_ref[...], axis=-1, keepdims=True))
           - jnp.exp(jnp.sum(lq2_ref[...] * lk2_ref[...], axis=-1, keepdims=True))
           + LAM_INIT)
    q = q_ref[0]
    k = k_ref[0]
    kc = kc_ref[0]
    lane = lax.broadcasted_iota(jnp.int32, q.shape, 1)

    def softmax_parts(first):
        qm = jnp.where((lane < HEAD_DIM) == first, q, jnp.zeros_like(q))
        s_l = _dot_nt(qm, k)
        s_c = _dot_nt(qm, kc)
        m = jnp.maximum(jnp.max(s_l, axis=-1, keepdims=True),
                        jnp.max(s_c, axis=-1, keepdims=True))
        p_l = jnp.exp2(s_l - m)
        p_c = jnp.exp2(s_c - m)
        denom = jnp.sum(p_l, axis=-1, keepdims=True) + jnp.sum(p_c, axis=-1, keepdims=True)
        return p_l, p_c, denom

    p0_l, p0_c, d0 = softmax_parts(True)
    p1_l, p1_c, d1 = softmax_parts(False)
    r0 = 1.0 / d0
    r1 = lam / d1
    a_l = (p0_l * r0 - p1_l * r1).astype(BF16)
    a_c = (p0_c * r0 - p1_c * r1).astype(BF16)
    o = _dot(a_l, v_ref[0]) + _dot(a_c, vc_ref[0])
    o = _rms(o) * gs_ref[...] * (1.0 - LAM_INIT)
    o_ref[0] = o.astype(o_ref.dtype)


def diff_attn(p3, pc3, lam_vecs, g_subln, *, n_heads, tq):
    b, l, _ = p3.shape
    ctx = pc3.shape[1]
    h = n_heads
    vec = pl.BlockSpec((1, HEAD_DIM), lambda bi, hi, qi: (0, 0))
    return pl.pallas_call(
        _attn_kernel,
        grid=(b, h, l // tq),
        in_specs=[vec, vec, vec, vec,
                  pl.BlockSpec((1, HEAD_COLS), lambda bi, hi, qi: (0, 0)),
                  pl.BlockSpec((1, tq, HEAD_COLS), lambda bi, hi, qi: (bi, qi, hi)),
                  pl.BlockSpec((1, l, HEAD_COLS), lambda bi, hi, qi: (bi, 0, h + hi)),
                  pl.BlockSpec((1, l, HEAD_COLS), lambda bi, hi, qi: (bi, 0, 2 * h + hi)),
                  pl.BlockSpec((1, ctx, HEAD_COLS), lambda bi, hi, qi: (bi, 0, hi)),
                  pl.BlockSpec((1, ctx, HEAD_COLS), lambda bi, hi, qi: (bi, 0, h + hi))],
        out_specs=pl.BlockSpec((1, tq, HEAD_COLS), lambda bi, hi, qi: (bi, qi, hi)),
        out_shape=jax.ShapeDtypeStruct((b, l, h * HEAD_COLS), BF16),
        compiler_params=_params(("arbitrary", "arbitrary", "arbitrary")),
        name="diff_attn",
    )(*lam_vecs, g_subln.reshape(1, HEAD_COLS), p3, p3, p3, pc3, pc3)


def dft_tables(n, scale):
    idx = (lax.broadcasted_iota(jnp.int32, (n, n), 0) * lax.broadcasted_iota(jnp.int32, (n, n), 1)) % n
    ang = idx.astype(F32) * (2.0 * math.pi / n)
    return (jnp.cos(ang) * scale).astype(BF16), (jnp.sin(ang) * scale).astype(BF16)


def _chan_dft_kernel(u_ref, c_ref, s_ref, uc_ref, us_ref):
    u = u_ref[...]
    uc_ref[...] = _dot(u, c_ref[...]).astype(uc_ref.dtype)
    us_ref[...] = _dot(u, s_ref[...]).astype(us_ref.dtype)


def chan_dft(p2d, cos_c, sin_c, *, col_tile0, tm):
    m = p2d.shape[0]
    gd = cos_c.shape[0]
    out = jax.ShapeDtypeStruct((m, N_FOURIER_GROUPS * gd), BF16)
    return pl.pallas_call(
        _chan_dft_kernel,
        grid=(m // tm, N_FOURIER_GROUPS),
        in_specs=[pl.BlockSpec((tm, gd), lambda i, g: (i, col_tile0 + g)),
                  _resident((gd, gd)), _resident((gd, gd))],
        out_specs=[pl.BlockSpec((tm, gd), lambda i, g: (i, g)),
                   pl.BlockSpec((tm, gd), lambda i, g: (i, g))],
        out_shape=[out, out],
        compiler_params=_params(("arbitrary", "arbitrary")),
        name="chan_dft",
    )(p2d, cos_c, sin_c)


def _pos_dft_kernel(c_ref, s_ref, uc_ref, us_ref, o_ref):
    y = _dot(c_ref[...], uc_ref[0]) - _dot(s_ref[...], us_ref[0])
    o_ref[0] = y.astype(o_ref.dtype)


def pos_dft(uc3, us3, cos_l, sin_l, *, tk, tn):
    b, l, w = uc3.shape
    return pl.pallas_call(
        _pos_dft_kernel,
        grid=(b, w // tn, l // tk),
        in_specs=[pl.BlockSpec((tk, l), lambda bi, ni, ki: (ki, 0)),
                  pl.BlockSpec((tk, l), lambda bi, ni, ki: (ki, 0)),
                  pl.BlockSpec((1, l, tn), lambda bi, ni, ki: (bi, 0, ni)),
                  pl.BlockSpec((1, l, tn), lambda bi, ni, ki: (bi, 0, ni))],
        out_specs=pl.BlockSpec((1, tk, tn), lambda bi, ni, ki: (bi, ki, ni)),
        out_shape=jax.ShapeDtypeStruct((b, l, w), BF16),
        compiler_params=_params(("arbitrary", "arbitrary", "arbitrary")),
        name="pos_dft",
    )(cos_l, sin_l, uc3, us3)


def _merge_kernel(a_ref, f_ref, ga_ref, gf_ref, x_ref, gt_ref, wa_ref, wf_ref, wo_ref, o_ref):
    ya = _dot(a_ref[...], wa_ref[...])
    yf = _dot(f_ref[...], wf_ref[...])
    merged = ga_ref[...].astype(F32) * ya + gf_ref[...].astype(F32) * yf
    y = _dot(merged.astype(BF16), wo_ref[...])
    o_ref[...] = x_ref[...] + gt_ref[0] * y


def merge(attn2d, four2d, p2d, x2d, mod3, wa, wf, wo, *, rows_per_mod, gate_col0, tm):
    m, d = x2d.shape
    dw = attn2d.shape[1]
    tiles_per_mod = rows_per_mod // tm
    ga_blk = gate_col0 // d
    return pl.pallas_call(
        _merge_kernel,
        grid=(m // tm,),
        in_specs=[pl.BlockSpec((tm, dw), lambda i: (i, 0)),
                  pl.BlockSpec((tm, dw), lambda i: (i, 0)),
                  pl.BlockSpec((tm, d), lambda i: (i, ga_blk)),
                  pl.BlockSpec((tm, d), lambda i: (i, ga_blk + 1)),
                  pl.BlockSpec((tm, d), lambda i: (i, 0)),
                  pl.BlockSpec((1, 1, d), lambda i: ((i // tiles_per_mod) * N_MOD + 2, 0, 0)),
                  _resident(wa.shape), _resident(wf.shape), _resident(wo.shape)],
        out_specs=pl.BlockSpec((tm, d), lambda i: (i, 0)),
        out_shape=jax.ShapeDtypeStruct((m, d), F32),
        compiler_params=_params(("arbitrary",)),
        name="merge",
    )(attn2d, four2d, p2d, p2d, x2d, mod3, wa, wf, wo)


def _mlp_kernel(x_ref, g_ref, sh_ref, sc_ref, gt_ref, gf_ref, w1_ref, w2_ref, o_ref, h_ref, acc_ref):
    j = pl.program_id(1)

    @pl.when(j == 0)
    def _():
        h = _norm_mod(x_ref[...], g_ref[...], sh_ref[0], sc_ref[0])
        h_ref[...] = h.astype(BF16)
        acc_ref[...] = jnp.zeros_like(acc_ref)

    z = jnp.maximum(_dot(h_ref[...], w1_ref[...]), 0.0)
    acc_ref[...] += _dot((z * z).astype(BF16), w2_ref[...])

    @pl.when(j == pl.num_programs(1) - 1)
    def _():
        y = x_ref[...] + gt_ref[0] * acc_ref[...]
        o_ref[...] = _rms(y) * gf_ref[...]


def mlp(x2d, g2, mod3, g_final, w1, w2, *, rows_per_mod, tm, tf):
    m, d = x2d.shape
    f = w1.shape[1]
    tiles_per_mod = rows_per_mod // tm

    def mod_map(which):
        return lambda i, j: ((i // tiles_per_mod) * N_MOD + which, 0, 0)

    return pl.pallas_call(
        _mlp_kernel,
        grid=(m // tm, f // tf),
        in_specs=[pl.BlockSpec((tm, d), lambda i, j: (i, 0)),
                  pl.BlockSpec((1, d), lambda i, j: (0, 0)),
                  pl.BlockSpec((1, 1, d), mod_map(3)),
                  pl.BlockSpec((1, 1, d), mod_map(4)),
                  pl.BlockSpec((1, 1, d), mod_map(5)),
                  pl.BlockSpec((1, d), lambda i, j: (0, 0)),
                  pl.BlockSpec((d, tf), lambda i, j: (0, j)),
                  pl.BlockSpec((tf, d), lambda i, j: (j, 0))],
        out_specs=pl.BlockSpec((tm, d), lambda i, j: (i, 0)),
        out_shape=jax.ShapeDtypeStruct((m, d), F32),
        scratch_shapes=[pltpu.VMEM((tm, d), BF16), pltpu.VMEM((tm, d), F32)],
        compiler_params=_params(("arbitrary", "arbitrary")),
        name="mlp",
    )(x2d, g2.reshape(1, d), mod3, mod3, mod3, g_final.reshape(1, d), w1, w2)


def kernel(x, c, ctx, c_ctx, w_ada, b_ada, g_norm1, w_in, lam_q1, lam_k1, lam_q2, lam_k2, g_subln,
           w_attn_br, w_four_br, w_out, g_norm2, w_mlp_in, w_mlp_out, g_final):
    b, l, d = x.shape
    n_ctx = ctx.shape[1]
    assert w_ada.shape[0] == 1, "single-layer block"
    dw = w_attn_br.shape[1]
    fw = w_four_br.shape[1]
    n_heads = dw // HEAD_COLS
    gd = fw // N_FOURIER_GROUPS
    in_cols = w_in.shape[2]
    gate_col0 = 3 * dw + fw
    assert in_cols == gate_col0 + 2 * d

    cvec = jnp.zeros((8, d), F32).at[:b].set(c).at[b].set(c_ctx)
    mod = ada_mod(cvec, w_ada[0], b_ada[0])
    mod3 = mod[:b + 1].reshape((b + 1) * N_MOD, 1, d)

    w_in_bf = w_in[0].astype(BF16)
    tabs = rope_tables(l)
    tm, tn = 512, 512
    x2d = x.reshape(b * l, d)
    p2d = in_proj(x2d, g_norm1[0], mod3, 0, w_in_bf, tabs, rows_per_mod=l, col_tile0=0,
                  n_cols=in_cols, rope_tiles=2 * dw // tn, gate_tile0=gate_col0 // tn, tm=tm, tn=tn)
    pc2d = in_proj(ctx.reshape(b * n_ctx, d), g_norm1[0], mod3, b, w_in_bf, tabs,
                   rows_per_mod=b * n_ctx, col_tile0=dw // tn, n_cols=2 * dw,
                   rope_tiles=0, gate_tile0=0, tm=b * n_ctx, tn=tn)

    lam_vecs = [v[0].astype(F32).reshape(1, HEAD_DIM) for v in (lam_q1, lam_k1, lam_q2, lam_k2)]
    attn = diff_attn(p2d.reshape(b, l, in_cols), pc2d.reshape(b, n_ctx, 2 * dw), lam_vecs,
                     g_subln[0], n_heads=n_heads, tq=256)

    cos_c, sin_c = dft_tables(gd, gd ** -0.5)
    cos_l, sin_l = dft_tables(l, l ** -0.5)
    uc, us = chan_dft(p2d, cos_c, sin_c, col_tile0=3 * dw // gd, tm=1024)
    four = pos_dft(uc.reshape(b, l, fw), us.reshape(b, l, fw), cos_l, sin_l, tk=512, tn=512)

    x_mid = merge(attn.reshape(b * l, dw), four.reshape(b * l, fw), p2d, x2d, mod3,
                  w_attn_br[0].astype(BF16), w_four_br[0].astype(BF16), w_out[0].astype(BF16),
                  rows_per_mod=l, gate_col0=gate_col0, tm=256)
    out = mlp(x_mid, g_norm2[0], mod3, g_final, w_mlp_in[0].astype(BF16), w_mlp_out[0].astype(BF16),
              rows_per_mod=l, tm=512, tf=1024)
    return out.reshape(b, l, d)
```

```python
import functools
import math

import jax
import jax.numpy as jnp
import numpy as np
from jax import lax
from jax.experimental import pallas as pl
from jax.experimental.pallas import tpu as pltpu

GRID_W = 64
HEAD_DIM = 64
HEAD_COLS = 2 * HEAD_DIM
N_FOURIER_GROUPS = 4
N_MOD = 6
ROPE_BASE = 10000.0
EPS = 1e-6
LAM_INIT = 0.8 - 0.6 * math.exp(-0.3 * 0)

V7X_VMEM_LIMIT_BYTES = 56 * 1024 * 1024
BF16 = jnp.bfloat16
F32 = jnp.float32


def _params(semantics):
    return pltpu.CompilerParams(dimension_semantics=semantics,
                                vmem_limit_bytes=V7X_VMEM_LIMIT_BYTES)


def _dot(a, b):
    return jnp.dot(a, b, preferred_element_type=F32)


def _dot_nt(a, b):
    return lax.dot_general(a, b, (((1,), (1,)), ((), ())), preferred_element_type=F32)


def _resident(shape):
    zeros = (0,) * len(shape)
    return pl.BlockSpec(shape, lambda *_: zeros, pipeline_mode=pl.Buffered(1))


def _ada_kernel(c_ref, w_ref, b_ref, o_ref):
    cv = c_ref[...]
    o_ref[...] = _dot(cv * jax.nn.sigmoid(cv), w_ref[...]) + b_ref[...]


def ada_mod(cvec, w_ada, b_ada, *, tn=1024):
    rows, d = cvec.shape
    n = w_ada.shape[1]
    return pl.pallas_call(
        _ada_kernel,
        grid=(n // tn,),
        in_specs=[pl.BlockSpec((rows, d), lambda j: (0, 0)),
                  pl.BlockSpec((d, tn), lambda j: (0, j)),
                  pl.BlockSpec((1, tn), lambda j: (0, j))],
        out_specs=pl.BlockSpec((rows, tn), lambda j: (0, j)),
        out_shape=jax.ShapeDtypeStruct((rows, n), F32),
        compiler_params=_params(("arbitrary",)),
        name="ada_mod",
    )(cvec, w_ada, b_ada.reshape(1, n))


def _rms(x):
    return x * lax.rsqrt(jnp.mean(x * x, axis=-1, keepdims=True) + EPS)


def _norm_mod(x, g, shift, scale):
    return _rms(x) * g * (1.0 + scale) + shift


def _in_proj_kernel(x_ref, g_ref, sh_ref, sc_ref, w_ref, tab_ref, o_ref, h_ref, *,
                    rope_tiles, gate_tile0):
    j = pl.program_id(1)

    @pl.when(j == 0)
    def _():
        h = _norm_mod(x_ref[...], g_ref[...], sh_ref[0], sc_ref[0])
        h_ref[...] = h.astype(BF16)

    acc = _dot(h_ref[...], w_ref[...])
    tn = acc.shape[1]

    if rope_tiles:
        @pl.when(j < rope_tiles)
        def _():
            tab = tab_ref[0]
            cos = tab[:, :HEAD_COLS]
            sin_lo = tab[:, HEAD_COLS:2 * HEAD_COLS]
            sin_hi = tab[:, 2 * HEAD_COLS:]
            half = HEAD_DIM // 4
            for c in range(tn // HEAD_COLS):
                xs = acc[:, c * HEAD_COLS:(c + 1) * HEAD_COLS]
                up = pltpu.roll(xs, HEAD_COLS - half, 1)
                dn = pltpu.roll(xs, half, 1)
                y = xs * cos + up * sin_lo + dn * sin_hi
                o_ref[:, c * HEAD_COLS:(c + 1) * HEAD_COLS] = y.astype(o_ref.dtype)

        @pl.when((j >= rope_tiles) & (j < gate_tile0))
        def _():
            o_ref[...] = acc.astype(o_ref.dtype)

        @pl.when(j >= gate_tile0)
        def _():
            o_ref[...] = jax.nn.sigmoid(acc).astype(o_ref.dtype)
    else:
        o_ref[...] = acc.astype(o_ref.dtype)


def in_proj(x2d, g, mod3, mod_row0, w_bf, tabs, *, rows_per_mod, col_tile0, n_cols,
            rope_tiles, gate_tile0, tm, tn):
    m, d = x2d.shape
    tiles_per_mod = rows_per_mod // tm
    q_tiles = rope_tiles // 2 if rope_tiles else 1
    ltiles = tabs.shape[1] // tm if rope_tiles else 1

    def mod_map(which):
        return lambda i, j: ((mod_row0 + i // tiles_per_mod) * N_MOD + which, 0, 0)

    if rope_tiles:
        tab_spec = pl.BlockSpec((1, tm, 3 * HEAD_COLS),
                                lambda i, j: (jnp.minimum(j // q_tiles, 1), i % ltiles, 0))
    else:
        tab_spec = pl.BlockSpec((1, 8, 3 * HEAD_COLS), lambda i, j: (0, 0, 0))

    kern = functools.partial(_in_proj_kernel, rope_tiles=rope_tiles, gate_tile0=gate_tile0)
    return pl.pallas_call(
        kern,
        grid=(m // tm, n_cols // tn),
        in_specs=[pl.BlockSpec((tm, d), lambda i, j: (i, 0)),
                  pl.BlockSpec((1, d), lambda i, j: (0, 0)),
                  pl.BlockSpec((1, 1, d), mod_map(0)),
                  pl.BlockSpec((1, 1, d), mod_map(1)),
                  pl.BlockSpec((d, tn), lambda i, j: (0, j + col_tile0)),
                  tab_spec],
        out_specs=pl.BlockSpec((tm, tn), lambda i, j: (i, j)),
        out_shape=jax.ShapeDtypeStruct((m, n_cols), BF16),
        scratch_shapes=[pltpu.VMEM((tm, d), BF16)],
        compiler_params=_params(("arbitrary", "arbitrary")),
        name="in_proj",
    )(x2d, g.reshape(1, d), mod3, mod3, w_bf, tabs)


def rope_tables(seq_len):
    pos = np.arange(seq_len)
    row = pos // GRID_W
    col = pos % GRID_W
    half = HEAD_DIM // 2
    inv_freq = ROPE_BASE ** (-np.arange(0, half, 2, dtype=np.float64) / half)

    def cs(p):
        ang = p[:, None].astype(np.float64) * inv_freq[None, :]
        ang = np.concatenate([ang, ang], axis=-1)
        return np.cos(ang), np.sin(ang)

    cr, sr = cs(row)
    cc, sc = cs(col)
    cos64 = np.concatenate([cr, cc], axis=-1)
    sin64 = np.concatenate([sr, sc], axis=-1)
    lower = (np.arange(HEAD_DIM) % half) < (half // 2)
    sin_lo64 = np.where(lower[None, :], -sin64, 0.0)
    sin_hi64 = np.where(lower[None, :], 0.0, sin64)
    tab = np.concatenate([np.tile(cos64, (1, 2)), np.tile(sin_lo64, (1, 2)),
                          np.tile(sin_hi64, (1, 2))], axis=-1)
    q_scale = HEAD_DIM ** -0.5 * math.log2(math.e)
    return jnp.asarray(np.stack([tab * q_scale, tab]), dtype=F32)


def _attn_kernel(lq1_ref, lk1_ref, lq2_ref, lk2_ref, gs_ref, q_ref, k_ref, v_ref, kc_ref, vc_ref,
                 o_ref):
    lam = (jnp.exp(jnp.sum(lq1_ref[...] * lk1_ref[...], axis=-1, keepdims=True))
           - jnp.exp(jnp.sum(lq2_ref[...] * lk2_ref[...], axis=-1, keepdims=True))
           + LAM_INIT)
    q = q_ref[0]
    k = k_ref[0]
    kc = kc_ref[0]
    lane = lax.broadcasted_iota(jnp.int32, q.shape, 1)

    def softmax_parts(first):
        qm = jnp.where((lane < HEAD_DIM) == first, q, jnp.zeros_like(q))
        s_l = _dot_nt(qm, k)
        s_c = _dot_nt(qm, kc)
        m = jnp.maximum(jnp.max(s_l, axis=-1, keepdims=True),
                        jnp.max(s_c, axis=-1, keepdims=True))
        p_l = jnp.exp2(s_l - m)
        p_c = jnp.exp2(s_c - m)
        denom = jnp.sum(p_l, axis=-1, keepdims=True) + jnp.sum(p_c, axis=-1, keepdims=True)
        return p_l, p_c, denom

    p0_l, p0_c, d0 = softmax_parts(True)
    p1_l, p1_c, d1 = softmax_parts(False)
    r0 = 1.0 / d0
    r1 = lam / d1
    a_l = (p0_l * r0 - p1_l * r1).astype(BF16)
    a_c = (p0_c * r0 - p1_c * r1).astype(BF16)
    o = _dot(a_l, v_ref[0]) + _dot(a_c, vc_ref[0])
    o = _rms(o) * gs_ref[...] * (1.0 - LAM_INIT)
    o_ref[0] = o.astype(o_ref.dtype)


def diff_attn(p3, pc3, lam_vecs, g_subln, *, n_heads, tq):
    b, l, _ = p3.shape
    ctx = pc3.shape[1]
    h = n_heads
    vec = pl.BlockSpec((1, HEAD_DIM), lambda bi, hi, qi: (0, 0))
    return pl.pallas_call(
        _attn_kernel,
        grid=(b, h, l // tq),
        in_specs=[vec, vec, vec, vec,
                  pl.BlockSpec((1, HEAD_COLS), lambda bi, hi, qi: (0, 0)),
                  pl.BlockSpec((1, tq, HEAD_COLS), lambda bi, hi, qi: (bi, qi, hi)),
                  pl.BlockSpec((1, l, HEAD_COLS), lambda bi, hi, qi: (bi, 0, h + hi)),
                  pl.BlockSpec((1, l, HEAD_COLS), lambda bi, hi, qi: (bi, 0, 2 * h + hi)),
                  pl.BlockSpec((1, ctx, HEAD_COLS), lambda bi, hi, qi: (bi, 0, hi)),
                  pl.BlockSpec((1, ctx, HEAD_COLS), lambda bi, hi, qi: (bi, 0, h + hi))],
        out_specs=pl.BlockSpec((1, tq, HEAD_COLS), lambda bi, hi, qi: (bi, qi, hi)),
        out_shape=jax.ShapeDtypeStruct((b, l, h * HEAD_COLS), BF16),
        compiler_params=_params(("arbitrary", "arbitrary", "arbitrary")),
        name="diff_attn",
    )(*lam_vecs, g_subln.reshape(1, HEAD_COLS), p3, p3, p3, pc3, pc3)


def dft_tables(n, scale):
    idx = (lax.broadcasted_iota(jnp.int32, (n, n), 0) * lax.broadcasted_iota(jnp.int32, (n, n), 1)) % n
    ang = idx.astype(F32) * (2.0 * math.pi / n)
    return (jnp.cos(ang) * scale).astype(BF16), (jnp.sin(ang) * scale).astype(BF16)


def _chan_dft_kernel(u_ref, c_ref, s_ref, uc_ref, us_ref):
    u = u_ref[...]
    uc_ref[...] = _dot(u, c_ref[...]).astype(uc_ref.dtype)
    us_ref[...] = _dot(u, s_ref[...]).astype(us_ref.dtype)


def chan_dft(p2d, cos_c, sin_c, *, col_tile0, tm):
    m = p2d.shape[0]
    gd = cos_c.shape[0]
    out = jax.ShapeDtypeStruct((m, N_FOURIER_GROUPS * gd), BF16)
    return pl.pallas_call(
        _chan_dft_kernel,
        grid=(m // tm, N_FOURIER_GROUPS),
        in_specs=[pl.BlockSpec((tm, gd), lambda i, g: (i, col_tile0 + g)),
                  _resident((gd, gd)), _resident((gd, gd))],
        out_specs=[pl.BlockSpec((tm, gd), lambda i, g: (i, g)),
                   pl.BlockSpec((tm, gd), lambda i, g: (i, g))],
        out_shape=[out, out],
        compiler_params=_params(("arbitrary", "arbitrary")),
        name="chan_dft",
    )(p2d, cos_c, sin_c)


def _pos_dft_kernel(c_ref, s_ref, uc_ref, us_ref, o_ref):
    y = _dot(c_ref[...], uc_ref[0]) - _dot(s_ref[...], us_ref[0])
    o_ref[0] = y.astype(o_ref.dtype)


def pos_dft(uc3, us3, cos_l, sin_l, *, tk, tn):
    b, l, w = uc3.shape
    return pl.pallas_call(
        _pos_dft_kernel,
        grid=(b, w // tn, l // tk),
        in_specs=[pl.BlockSpec((tk, l), lambda bi, ni, ki: (ki, 0)),
                  pl.BlockSpec((tk, l), lambda bi, ni, ki: (ki, 0)),
                  pl.BlockSpec((1, l, tn), lambda bi, ni, ki: (bi, 0, ni)),
                  pl.BlockSpec((1, l, tn), lambda bi, ni, ki: (bi, 0, ni))],
        out_specs=pl.BlockSpec((1, tk, tn), lambda bi, ni, ki: (bi, ki, ni)),
        out_shape=jax.ShapeDtypeStruct((b, l, w), BF16),
        compiler_params=_params(("arbitrary", "arbitrary", "arbitrary")),
        name="pos_dft",
    )(cos_l, sin_l, uc3, us3)


def _merge_kernel(a_ref, f_ref, ga_ref, gf_ref, x_ref, gt_ref, wa_ref, wf_ref, wo_ref, o_ref):
    ya = _dot(a_ref[...], wa_ref[...])
    yf = _dot(f_ref[...], wf_ref[...])
    merged = ga_ref[...].astype(F32) * ya + gf_ref[...].astype(F32) * yf
    y = _dot(merged.astype(BF16), wo_ref[...])
    o_ref[...] = x_ref[...] + gt_ref[0] * y


def merge(attn2d, four2d, p2d, x2d, mod3, wa, wf, wo, *, rows_per_mod, gate_col0, tm):
    m, d = x2d.shape
    dw = attn2d.shape[1]
    tiles_per_mod = rows_per_mod // tm
    ga_blk = gate_col0 // d
    return pl.pallas_call(
        _merge_kernel,
        grid=(m // tm,),
        in_specs=[pl.BlockSpec((tm, dw), lambda i: (i, 0)),
                  pl.BlockSpec((tm, dw), lambda i: (i, 0)),
                  pl.BlockSpec((tm, d), lambda i: (i, ga_blk)),
                  pl.BlockSpec((tm, d), lambda i: (i, ga_blk + 1)),
                  pl.BlockSpec((tm, d), lambda i: (i, 0)),
                  pl.BlockSpec((1, 1, d), lambda i: ((i // tiles_per_mod) * N_MOD + 2, 0, 0)),
                  _resident(wa.shape), _resident(wf.shape), _resident(wo.shape)],
        out_specs=pl.BlockSpec((tm, d), lambda i: (i, 0)),
        out_shape=jax.ShapeDtypeStruct((m, d), F32),
        compiler_params=_params(("arbitrary",)),
        name="merge",
    )(attn2d, four2d, p2d, p2d, x2d, mod3, wa, wf, wo)


def _mlp_kernel(x_ref, g_ref, sh_ref, sc_ref, gt_ref, gf_ref, w1_ref, w2_ref, o_ref, h_ref, acc_ref):
    j = pl.program_id(1)

    @pl.when(j == 0)
    def _():
        h = _norm_mod(x_ref[...], g_ref[...], sh_ref[0], sc_ref[0])
        h_ref[...] = h.astype(BF16)
        acc_ref[...] = jnp.zeros_like(acc_ref)

    z = jnp.maximum(_dot(h_ref[...], w1_ref[...]), 0.0)
    acc_ref[...] += _dot((z * z).astype(BF16), w2_ref[...])

    @pl.when(j == pl.num_programs(1) - 1)
    def _():
        y = x_ref[...] + gt_ref[0] * acc_ref[...]
        o_ref[...] = _rms(y) * gf_ref[...]


def mlp(x2d, g2, mod3, g_final, w1, w2, *, rows_per_mod, tm, tf):
    m, d = x2d.shape
    f = w1.shape[1]
    tiles_per_mod = rows_per_mod // tm

    def mod_map(which):
        return lambda i, j: ((i // tiles_per_mod) * N_MOD + which, 0, 0)

    return pl.pallas_call(
        _mlp_kernel,
        grid=(m // tm, f // tf),
        in_specs=[pl.BlockSpec((tm, d), lambda i, j: (i, 0)),
                  pl.BlockSpec((1, d), lambda i, j: (0, 0)),
                  pl.BlockSpec((1, 1, d), mod_map(3)),
                  pl.BlockSpec((1, 1, d), mod_map(4)),
                  pl.BlockSpec((1, 1, d), mod_map(5)),
                  pl.BlockSpec((1, d), lambda i, j: (0, 0)),
                  pl.BlockSpec((d, tf), lambda i, j: (0, j)),
                  pl.BlockSpec((tf, d), lambda i, j: (j, 0))],
        out_specs=pl.BlockSpec((tm, d), lambda i, j: (i, 0), pipeline_mode=pl.Buffered(1)),
        out_shape=jax.ShapeDtypeStruct((m, d), F32),
        scratch_shapes=[pltpu.VMEM((tm, d), BF16), pltpu.VMEM((tm, d), F32)],
        compiler_params=_params(("arbitrary", "arbitrary")),
        name="mlp",
    )(x2d, g2.reshape(1, d), mod3, mod3, mod3, g_final.reshape(1, d), w1, w2)


def kernel(x, c, ctx, c_ctx, w_ada, b_ada, g_norm1, w_in, lam_q1, lam_k1, lam_q2, lam_k2, g_subln,
           w_attn_br, w_four_br, w_out, g_norm2, w_mlp_in, w_mlp_out, g_final):
    b, l, d = x.shape
    n_ctx = ctx.shape[1]
    assert w_ada.shape[0] == 1, "single-layer block"
    dw = w_attn_br.shape[1]
    fw = w_four_br.shape[1]
    n_heads = dw // HEAD_COLS
    gd = fw // N_FOURIER_GROUPS
    in_cols = w_in.shape[2]
    gate_col0 = 3 * dw + fw
    assert in_cols == gate_col0 + 2 * d

    cvec = jnp.zeros((8, d), F32).at[:b].set(c).at[b].set(c_ctx)
    mod = ada_mod(cvec, w_ada[0], b_ada[0])
    mod3 = mod[:b + 1].reshape((b + 1) * N_MOD, 1, d)

    w_in_bf = w_in[0].astype(BF16)
    tabs = rope_tables(l)
    tm, tn = 1024, 1024
    x2d = x.reshape(b * l, d)
    p2d = in_proj(x2d, g_norm1[0], mod3, 0, w_in_bf, tabs, rows_per_mod=l, col_tile0=0,
                  n_cols=in_cols, rope_tiles=2 * dw // tn, gate_tile0=gate_col0 // tn, tm=tm, tn=tn)
    pc2d = in_proj(ctx.reshape(b * n_ctx, d), g_norm1[0], mod3, b, w_in_bf, tabs,
                   rows_per_mod=b * n_ctx, col_tile0=dw // tn, n_cols=2 * dw,
                   rope_tiles=0, gate_tile0=0, tm=b * n_ctx, tn=tn)

    lam_vecs = [v[0].astype(F32).reshape(1, HEAD_DIM) for v in (lam_q1, lam_k1, lam_q2, lam_k2)]
    attn = diff_attn(p2d.reshape(b, l, in_cols), pc2d.reshape(b, n_ctx, 2 * dw), lam_vecs,
                     g_subln[0], n_heads=n_heads, tq=256)

    cos_c, sin_c = dft_tables(gd, gd ** -0.5)
    cos_l, sin_l = dft_tables(l, l ** -0.5)
    uc, us = chan_dft(p2d, cos_c, sin_c, col_tile0=3 * dw // gd, tm=1024)
    four = pos_dft(uc.reshape(b, l, fw), us.reshape(b, l, fw), cos_l, sin_l, tk=512, tn=512)

    x_mid = merge(attn.reshape(b * l, dw), four.reshape(b * l, fw), p2d, x2d, mod3,
                  w_attn_br[0].astype(BF16), w_four_br[0].astype(BF16), w_out[0].astype(BF16),
                  rows_per_mod=l, gate_col0=gate_col0, tm=256)
    out = mlp(x_mid, g_norm2[0], mod3, g_final, w_mlp_in[0].astype(BF16), w_mlp_out[0].astype(BF16),
              rows_per_mod=l, tm=1024, tf=256)
    return out.reshape(b, l, d)
```

```python
import functools
import math

import jax
import jax.numpy as jnp
import numpy as np
from jax import lax
from jax.experimental import pallas as pl
from jax.experimental.pallas import tpu as pltpu

GRID_W = 64
HEAD_DIM = 64
HEAD_COLS = 2 * HEAD_DIM
N_FOURIER_GROUPS = 4
N_MOD = 6
ROPE_BASE = 10000.0
EPS = 1e-6
LAM_INIT = 0.8 - 0.6 * math.exp(-0.3 * 0)

V7X_VMEM_LIMIT_BYTES = 56 * 1024 * 1024
BF16 = jnp.bfloat16
F32 = jnp.float32


def _params(semantics):
    return pltpu.CompilerParams(dimension_semantics=semantics,
                                vmem_limit_bytes=V7X_VMEM_LIMIT_BYTES)


def _dot(a, b):
    return jnp.dot(a, b, preferred_element_type=F32)


def _dot_nt(a, b):
    return lax.dot_general(a, b, (((1,), (1,)), ((), ())), preferred_element_type=F32)


def _resident(shape):
    zeros = (0,) * len(shape)
    return pl.BlockSpec(shape, lambda *_: zeros, pipeline_mode=pl.Buffered(1))


def _ada_kernel(c_ref, w_ref, b_ref, o_ref):
    cv = c_ref[...]
    o_ref[...] = _dot(cv * jax.nn.sigmoid(cv), w_ref[...]) + b_ref[...]


def ada_mod(cvec, w_ada, b_ada, *, tn=1024):
    rows, d = cvec.shape
    n = w_ada.shape[1]
    return pl.pallas_call(
        _ada_kernel,
        grid=(n // tn,),
        in_specs=[pl.BlockSpec((rows, d), lambda j: (0, 0)),
                  pl.BlockSpec((d, tn), lambda j: (0, j)),
                  pl.BlockSpec((1, tn), lambda j: (0, j))],
        out_specs=pl.BlockSpec((rows, tn), lambda j: (0, j)),
        out_shape=jax.ShapeDtypeStruct((rows, n), F32),
        compiler_params=_params(("arbitrary",)),
        name="ada_mod",
    )(cvec, w_ada, b_ada.reshape(1, n))


def _rms(x):
    return x * lax.rsqrt(jnp.mean(x * x, axis=-1, keepdims=True) + EPS)


def _norm_mod(x, g, shift, scale):
    return _rms(x) * g * (1.0 + scale) + shift


def _in_proj_kernel(x_ref, g_ref, sh_ref, sc_ref, w_ref, tab_ref, o_ref, h_ref, *,
                    rope_tiles, gate_tile0):
    j = pl.program_id(1)

    @pl.when(j == 0)
    def _():
        h = _norm_mod(x_ref[...], g_ref[...], sh_ref[0], sc_ref[0])
        h_ref[...] = h.astype(BF16)

    acc = _dot(h_ref[...], w_ref[...])
    tn = acc.shape[1]

    if rope_tiles:
        @pl.when(j < rope_tiles)
        def _():
            tab = tab_ref[0]
            cos = tab[:, :HEAD_COLS]
            sin_lo = tab[:, HEAD_COLS:2 * HEAD_COLS]
            sin_hi = tab[:, 2 * HEAD_COLS:]
            half = HEAD_DIM // 4
            for c in range(tn // HEAD_COLS):
                xs = acc[:, c * HEAD_COLS:(c + 1) * HEAD_COLS]
                up = pltpu.roll(xs, HEAD_COLS - half, 1)
                dn = pltpu.roll(xs, half, 1)
                y = xs * cos + up * sin_lo + dn * sin_hi
                o_ref[:, c * HEAD_COLS:(c + 1) * HEAD_COLS] = y.astype(o_ref.dtype)

        @pl.when((j >= rope_tiles) & (j < gate_tile0))
        def _():
            o_ref[...] = acc.astype(o_ref.dtype)

        @pl.when(j >= gate_tile0)
        def _():
            o_ref[...] = jax.nn.sigmoid(acc).astype(o_ref.dtype)
    else:
        o_ref[...] = acc.astype(o_ref.dtype)


def in_proj(x2d, g, mod3, mod_row0, w_bf, tabs, *, rows_per_mod, col_tile0, n_cols,
            rope_tiles, gate_tile0, tm, tn):
    m, d = x2d.shape
    tiles_per_mod = rows_per_mod // tm
    q_tiles = rope_tiles // 2 if rope_tiles else 1
    ltiles = tabs.shape[1] // tm if rope_tiles else 1

    def mod_map(which):
        return lambda i, j: ((mod_row0 + i // tiles_per_mod) * N_MOD + which, 0, 0)

    if rope_tiles:
        tab_spec = pl.BlockSpec((1, tm, 3 * HEAD_COLS),
                                lambda i, j: (jnp.minimum(j // q_tiles, 1), i % ltiles, 0))
    else:
        tab_spec = pl.BlockSpec((1, 8, 3 * HEAD_COLS), lambda i, j: (0, 0, 0))

    kern = functools.partial(_in_proj_kernel, rope_tiles=rope_tiles, gate_tile0=gate_tile0)
    return pl.pallas_call(
        kern,
        grid=(m // tm, n_cols // tn),
        in_specs=[pl.BlockSpec((tm, d), lambda i, j: (i, 0)),
                  pl.BlockSpec((1, d), lambda i, j: (0, 0)),
                  pl.BlockSpec((1, 1, d), mod_map(0)),
                  pl.BlockSpec((1, 1, d), mod_map(1)),
                  pl.BlockSpec((d, tn), lambda i, j: (0, j + col_tile0)),
                  tab_spec],
        out_specs=pl.BlockSpec((tm, tn), lambda i, j: (i, j)),
        out_shape=jax.ShapeDtypeStruct((m, n_cols), BF16),
        scratch_shapes=[pltpu.VMEM((tm, d), BF16)],
        compiler_params=_params(("arbitrary", "arbitrary")),
        name="in_proj",
    )(x2d, g.reshape(1, d), mod3, mod3, w_bf, tabs)


def rope_tables(seq_len):
    pos = np.arange(seq_len)
    row = pos // GRID_W
    col = pos % GRID_W
    half = HEAD_DIM // 2
    inv_freq = ROPE_BASE ** (-np.arange(0, half, 2, dtype=np.float64) / half)

    def cs(p):
        ang = p[:, None].astype(np.float64) * inv_freq[None, :]
        ang = np.concatenate([ang, ang], axis=-1)
        return np.cos(ang), np.sin(ang)

    cr, sr = cs(row)
    cc, sc = cs(col)
    cos64 = np.concatenate([cr, cc], axis=-1)
    sin64 = np.concatenate([sr, sc], axis=-1)
    lower = (np.arange(HEAD_DIM) % half) < (half // 2)
    sin_lo64 = np.where(lower[None, :], -sin64, 0.0)
    sin_hi64 = np.where(lower[None, :], 0.0, sin64)
    tab = np.concatenate([np.tile(cos64, (1, 2)), np.tile(sin_lo64, (1, 2)),
                          np.tile(sin_hi64, (1, 2))], axis=-1)
    q_scale = HEAD_DIM ** -0.5 * math.log2(math.e)
    return jnp.asarray(np.stack([tab * q_scale, tab]), dtype=F32)


def _attn_kernel(lq1_ref, lk1_ref, lq2_ref, lk2_ref, gs_ref, q_ref, k_ref, v_ref, kc_ref, vc_ref,
                 o_ref):
    lam = (jnp.exp(jnp.sum(lq1_ref[...] * lk1_ref[...], axis=-1, keepdims=True))
           - jnp.exp(jnp.sum(lq2_ref[...] * lk2_ref[...], axis=-1, keepdims=True))
           + LAM_INIT)
    q = q_ref[0]
    k = k_ref[0]
    kc = kc_ref[0]
    lane = lax.broadcasted_iota(jnp.int32, q.shape, 1)

    def softmax_parts(first):
        qm = jnp.where((lane < HEAD_DIM) == first, q, jnp.zeros_like(q))
        s_l = _dot_nt(qm, k)
        s_c = _dot_nt(qm, kc)
        m = jnp.maximum(jnp.max(s_l, axis=-1, keepdims=True),
                        jnp.max(s_c, axis=-1, keepdims=True))
        p_l = jnp.exp2(s_l - m)
        p_c = jnp.exp2(s_c - m)
        denom = jnp.sum(p_l, axis=-1, keepdims=True) + jnp.sum(p_c, axis=-1, keepdims=True)
        return p_l, p_c, denom

    p0_l, p0_c, d0 = softmax_parts(True)
    p1_l, p1_c, d1 = softmax_parts(False)
    r0 = 1.0 / d0
    r1 = lam / d1
    a_l = (p0_l * r0 - p1_l * r1).astype(BF16)
    a_c = (p0_c * r0 - p1_c * r1).astype(BF16)
    o = _dot(a_l, v_ref[0]) + _dot(a_c, vc_ref[0])
    o = _rms(o) * gs_ref[...] * (1.0 - LAM_INIT)
    o_ref[0] = o.astype(o_ref.dtype)


def diff_attn(p3, pc3, lam_vecs, g_subln, *, n_heads, tq):
    b, l, _ = p3.shape
    ctx = pc3.shape[1]
    h = n_heads
    vec = pl.BlockSpec((1, HEAD_DIM), lambda bi, hi, qi: (0, 0))
    return pl.pallas_call(
        _attn_kernel,
        grid=(b, h, l // tq),
        in_specs=[vec, vec, vec, vec,
                  pl.BlockSpec((1, HEAD_COLS), lambda bi, hi, qi: (0, 0)),
                  pl.BlockSpec((1, tq, HEAD_COLS), lambda bi, hi, qi: (bi, qi, hi)),
                  pl.BlockSpec((1, l, HEAD_COLS), lambda bi, hi, qi: (bi, 0, h + hi)),
                  pl.BlockSpec((1, l, HEAD_COLS), lambda bi, hi, qi: (bi, 0, 2 * h + hi)),
                  pl.BlockSpec((1, ctx, HEAD_COLS), lambda bi, hi, qi: (bi, 0, hi)),
                  pl.BlockSpec((1, ctx, HEAD_COLS), lambda bi, hi, qi: (bi, 0, h + hi))],
        out_specs=pl.BlockSpec((1, tq, HEAD_COLS), lambda bi, hi, qi: (bi, qi, hi)),
        out_shape=jax.ShapeDtypeStruct((b, l, h * HEAD_COLS), BF16),
        compiler_params=_params(("arbitrary", "arbitrary", "arbitrary")),
        name="diff_attn",
    )(*lam_vecs, g_subln.reshape(1, HEAD_COLS), p3, p3, p3, pc3, pc3)


FOURIER_N1 = 16


def _bf16_const(a):
    return jnp.asarray(a, dtype=F32).astype(BF16)


def dft_tables(n, scale):
    idx = (np.arange(n)[:, None] * np.arange(n)[None, :]) % n
    ang = 2.0 * np.pi * idx / n
    return np.cos(ang) * scale, np.sin(ang) * scale


def stage_a_tables(l, n1, sub, scale):
    n2 = l // n1
    j = np.arange(n2 // sub)[:, None, None, None]
    k1 = np.arange(n1)[None, :, None, None]
    a = np.arange(sub)[None, None, :, None]
    m1 = np.arange(n1)[None, None, None, :]
    ang = 2.0 * np.pi * ((k1 * (m1 * n2 + j * sub + a)) % l) / l
    eye = np.eye(sub)
    c = np.einsum("jkam,ab->jkamb", np.cos(ang) * scale, eye).reshape(n2 // sub, n1 * sub, n1 * sub)
    s = np.einsum("jkam,ab->jkamb", np.sin(ang) * scale, eye).reshape(n2 // sub, n1 * sub, n1 * sub)
    return np.concatenate([c, -s], axis=1), np.concatenate([-s, -c], axis=1)


def _fourier_a_kernel(cc_ref, sc_ref, ka_ref, kb_ref, u_ref, yr_ref, yi_ref, *, gd):
    rows = ka_ref.shape[2]
    u = u_ref[0].reshape(rows, u_ref.shape[3])
    groups = range(u.shape[1] // gd)
    uc = jnp.concatenate([_dot(u[:, g * gd:(g + 1) * gd], cc_ref[...]) for g in groups], axis=1)
    us = jnp.concatenate([_dot(u[:, g * gd:(g + 1) * gd], sc_ref[...]) for g in groups], axis=1)
    y = _dot(ka_ref[0], uc.astype(BF16)) + _dot(kb_ref[0], us.astype(BF16))
    yr_ref[0] = y[:rows].astype(yr_ref.dtype).reshape(yr_ref.shape[1:])
    yi_ref[0] = y[rows:].astype(yi_ref.dtype).reshape(yi_ref.shape[1:])


def fourier_a(p4, cos_c, sin_c, ka, kb, *, col_block, width):
    b, n1, n2, _ = p4.shape
    gd = cos_c.shape[0]
    tiles, rows2, rows = ka.shape
    sub = rows // n1
    tab = pl.BlockSpec((1, rows2, rows), lambda bi, j: (j, 0, 0))
    out_spec = pl.BlockSpec((1, n1, sub, width), lambda bi, j: (bi, 0, j, 0))
    out = jax.ShapeDtypeStruct((b, n1, n2, width), BF16)
    return pl.pallas_call(
        functools.partial(_fourier_a_kernel, gd=gd),
        grid=(b, tiles),
        in_specs=[_resident((gd, gd)), _resident((gd, gd)), tab, tab,
                  pl.BlockSpec((1, n1, sub, width), lambda bi, j: (bi, 0, j, col_block))],
        out_specs=[out_spec, out_spec],
        out_shape=[out, out],
        compiler_params=_params(("arbitrary", "arbitrary")),
        name="fourier_a",
    )(cos_c, sin_c, ka, kb, p4)


def _fourier_c_kernel(c_ref, s_ref, yr_ref, yi_ref, o_ref):
    o = _dot(c_ref[...], yr_ref[0, 0]) + _dot(s_ref[...], yi_ref[0, 0])
    o_ref[0, 0] = o.astype(o_ref.dtype)


def fourier_c(yr4, yi4, cos2, sin2):
    b, n1, n2, width = yr4.shape
    data = pl.BlockSpec((1, 1, n2, width), lambda bi, k: (bi, k, 0, 0))
    return pl.pallas_call(
        _fourier_c_kernel,
        grid=(b, n1),
        in_specs=[_resident((n2, n2)), _resident((n2, n2)), data, data],
        out_specs=data,
        out_shape=jax.ShapeDtypeStruct((b, n1, n2, width), BF16),
        compiler_params=_params(("arbitrary", "arbitrary")),
        name="fourier_c",
    )(cos2, sin2, yr4, yi4)


def position_order_matrix(n1, sub):
    p = np.zeros((sub * n1, n1 * sub))
    for k1 in range(n1):
        for k2 in range(sub):
            p[k2 * n1 + k1, k1 * sub + k2] = 1.0
    return p


def _merge_kernel(a_ref, f_ref, ga_ref, gf_ref, x_ref, gt_ref, pm_ref, wa_ref, wf_ref, wo_ref, o_ref):
    ya = _dot(a_ref[...], wa_ref[...])
    f = f_ref[0].reshape(pm_ref.shape[1], f_ref.shape[3])
    yf = _dot(_dot(pm_ref[...], f).astype(BF16), wf_ref[...])
    merged = ga_ref[...].astype(F32) * ya + gf_ref[...].astype(F32) * yf
    y = _dot(merged.astype(BF16), wo_ref[...])
    o_ref[...] = x_ref[...] + gt_ref[0] * y


def merge(attn2d, four4, pm, p2d, x2d, mod3, wa, wf, wo, *, rows_per_mod, gate_col0):
    m, d = x2d.shape
    dw = attn2d.shape[1]
    _, n1, _, fw = four4.shape
    tm = pm.shape[0]
    sub = tm // n1
    tiles_per_mod = rows_per_mod // tm
    ga_blk = gate_col0 // d
    return pl.pallas_call(
        _merge_kernel,
        grid=(m // tm,),
        in_specs=[pl.BlockSpec((tm, dw), lambda i: (i, 0)),
                  pl.BlockSpec((1, n1, sub, fw), lambda i: (i // tiles_per_mod, 0, i % tiles_per_mod, 0)),
                  pl.BlockSpec((tm, d), lambda i: (i, ga_blk)),
                  pl.BlockSpec((tm, d), lambda i: (i, ga_blk + 1)),
                  pl.BlockSpec((tm, d), lambda i: (i, 0)),
                  pl.BlockSpec((1, 1, d), lambda i: ((i // tiles_per_mod) * N_MOD + 2, 0, 0)),
                  _resident(pm.shape), _resident(wa.shape), _resident(wf.shape), _resident(wo.shape)],
        out_specs=pl.BlockSpec((tm, d), lambda i: (i, 0)),
        out_shape=jax.ShapeDtypeStruct((m, d), F32),
        compiler_params=_params(("arbitrary",)),
        name="merge",
    )(attn2d, four4, p2d, p2d, x2d, mod3, pm, wa, wf, wo)


def _mlp_kernel(x_ref, g_ref, sh_ref, sc_ref, gt_ref, gf_ref, w1_ref, w2_ref, o_ref, h_ref, acc_ref):
    j = pl.program_id(1)

    @pl.when(j == 0)
    def _():
        h = _norm_mod(x_ref[...], g_ref[...], sh_ref[0], sc_ref[0])
        h_ref[...] = h.astype(BF16)
        acc_ref[...] = jnp.zeros_like(acc_ref)

    z = jnp.maximum(_dot(h_ref[...], w1_ref[...]), 0.0)
    acc_ref[...] += _dot((z * z).astype(BF16), w2_ref[...])

    @pl.when(j == pl.num_programs(1) - 1)
    def _():
        y = x_ref[...] + gt_ref[0] * acc_ref[...]
        o_ref[...] = _rms(y) * gf_ref[...]


def mlp(x2d, g2, mod3, g_final, w1, w2, *, rows_per_mod, tm, tf):
    m, d = x2d.shape
    f = w1.shape[1]
    tiles_per_mod = rows_per_mod // tm

    def mod_map(which):
        return lambda i, j: ((i // tiles_per_mod) * N_MOD + which, 0, 0)

    return pl.pallas_call(
        _mlp_kernel,
        grid=(m // tm, f // tf),
        in_specs=[pl.BlockSpec((tm, d), lambda i, j: (i, 0)),
                  pl.BlockSpec((1, d), lambda i, j: (0, 0)),
                  pl.BlockSpec((1, 1, d), mod_map(3)),
                  pl.BlockSpec((1, 1, d), mod_map(4)),
                  pl.BlockSpec((1, 1, d), mod_map(5)),
                  pl.BlockSpec((1, d), lambda i, j: (0, 0)),
                  pl.BlockSpec((d, tf), lambda i, j: (0, j)),
                  pl.BlockSpec((tf, d), lambda i, j: (j, 0))],
        out_specs=pl.BlockSpec((tm, d), lambda i, j: (i, 0), pipeline_mode=pl.Buffered(1)),
        out_shape=jax.ShapeDtypeStruct((m, d), F32),
        scratch_shapes=[pltpu.VMEM((tm, d), BF16), pltpu.VMEM((tm, d), F32)],
        compiler_params=_params(("arbitrary", "arbitrary")),
        name="mlp",
    )(x2d, g2.reshape(1, d), mod3, mod3, mod3, g_final.reshape(1, d), w1, w2)


def kernel(x, c, ctx, c_ctx, w_ada, b_ada, g_norm1, w_in, lam_q1, lam_k1, lam_q2, lam_k2, g_subln,
           w_attn_br, w_four_br, w_out, g_norm2, w_mlp_in, w_mlp_out, g_final):
    b, l, d = x.shape
    n_ctx = ctx.shape[1]
    assert w_ada.shape[0] == 1, "single-layer block"
    dw = w_attn_br.shape[1]
    fw = w_four_br.shape[1]
    n_heads = dw // HEAD_COLS
    gd = fw // N_FOURIER_GROUPS
    in_cols = w_in.shape[2]
    gate_col0 = 3 * dw + fw
    assert in_cols == gate_col0 + 2 * d

    cvec = jnp.zeros((8, d), F32).at[:b].set(c).at[b].set(c_ctx)
    mod = ada_mod(cvec, w_ada[0], b_ada[0])
    mod3 = mod[:b + 1].reshape((b + 1) * N_MOD, 1, d)

    w_in_bf = w_in[0].astype(BF16)
    tabs = rope_tables(l)
    tm, tn = 1024, 1024
    x2d = x.reshape(b * l, d)
    p2d = in_proj(x2d, g_norm1[0], mod3, 0, w_in_bf, tabs, rows_per_mod=l, col_tile0=0,
                  n_cols=in_cols, rope_tiles=2 * dw // tn, gate_tile0=gate_col0 // tn, tm=tm, tn=tn)
    pc2d = in_proj(ctx.reshape(b * n_ctx, d), g_norm1[0], mod3, b, w_in_bf, tabs,
                   rows_per_mod=b * n_ctx, col_tile0=dw // tn, n_cols=2 * dw,
                   rope_tiles=0, gate_tile0=0, tm=b * n_ctx, tn=tn)

    lam_vecs = [v[0].astype(F32).reshape(1, HEAD_DIM) for v in (lam_q1, lam_k1, lam_q2, lam_k2)]
    attn = diff_attn(p2d.reshape(b, l, in_cols), pc2d.reshape(b, n_ctx, 2 * dw), lam_vecs,
                     g_subln[0], n_heads=n_heads, tq=256)

    n1 = FOURIER_N1
    n2 = l // n1
    sub = FOURIER_N1
    assert n1 * n2 == l and n2 % sub == 0
    cos_c, sin_c = (_bf16_const(t) for t in dft_tables(gd, gd ** -0.5))
    ka, kb = (_bf16_const(t) for t in stage_a_tables(l, n1, sub, l ** -0.5))
    cos2, sin2 = (_bf16_const(t) for t in dft_tables(n2, 1.0))
    yr, yi = fourier_a(p2d.reshape(b, n1, n2, in_cols), cos_c, sin_c, ka, kb,
                       col_block=3 * dw // fw, width=fw)
    four4 = fourier_c(yr, yi, cos2, sin2)

    x_mid = merge(attn.reshape(b * l, dw), four4, _bf16_const(position_order_matrix(n1, sub)), p2d, x2d,
                  mod3, w_attn_br[0].astype(BF16), w_four_br[0].astype(BF16), w_out[0].astype(BF16),
                  rows_per_mod=l, gate_col0=gate_col0)
    out = mlp(x_mid, g_norm2[0], mod3, g_final, w_mlp_in[0].astype(BF16), w_mlp_out[0].astype(BF16),
              rows_per_mod=l, tm=1024, tf=512)
    return out.reshape(b, l, d)
```

```python
import functools
import math

import jax
import jax.numpy as jnp
import numpy as np
from jax import lax
from jax.experimental import pallas as pl
from jax.experimental.pallas import tpu as pltpu

GRID_W = 64
HEAD_DIM = 64
HEAD_COLS = 2 * HEAD_DIM
N_FOURIER_GROUPS = 4
N_MOD = 6
ROPE_BASE = 10000.0
EPS = 1e-6
LAM_INIT = 0.8 - 0.6 * math.exp(-0.3 * 0)

V7X_VMEM_LIMIT_BYTES = 56 * 1024 * 1024
BF16 = jnp.bfloat16
F32 = jnp.float32


def _params(semantics, flags=None):
    return pltpu.CompilerParams(dimension_semantics=semantics,
                                vmem_limit_bytes=V7X_VMEM_LIMIT_BYTES, flags=flags)


def _dot(a, b):
    return jnp.dot(a, b, preferred_element_type=F32)


def _dot_nt(a, b):
    return lax.dot_general(a, b, (((1,), (1,)), ((), ())), preferred_element_type=F32)


def _resident(shape):
    zeros = (0,) * len(shape)
    return pl.BlockSpec(shape, lambda *_: zeros, pipeline_mode=pl.Buffered(1))


def _ada_kernel(c_ref, w_ref, b_ref, o_ref):
    cv = c_ref[...]
    o_ref[...] = _dot(cv * jax.nn.sigmoid(cv), w_ref[...]) + b_ref[...]


def ada_mod(cvec, w_ada, b_ada, *, tn=1024):
    rows, d = cvec.shape
    n = w_ada.shape[1]
    return pl.pallas_call(
        _ada_kernel,
        grid=(n // tn,),
        in_specs=[pl.BlockSpec((rows, d), lambda j: (0, 0)),
                  pl.BlockSpec((d, tn), lambda j: (0, j)),
                  pl.BlockSpec((1, tn), lambda j: (0, j))],
        out_specs=pl.BlockSpec((rows, tn), lambda j: (0, j)),
        out_shape=jax.ShapeDtypeStruct((rows, n), F32),
        compiler_params=_params(("arbitrary",)),
        name="ada_mod",
    )(cvec, w_ada, b_ada.reshape(1, n))


def _rms(x):
    return x * lax.rsqrt(jnp.mean(x * x, axis=-1, keepdims=True) + EPS)


def _norm_mod(x, g, shift, scale):
    return _rms(x) * g * (1.0 + scale) + shift


def _in_proj_kernel(x_ref, g_ref, sh_ref, sc_ref, w_ref, tab_ref, o_ref, h_ref, *,
                    rope_tiles, gate_tile0):
    j = pl.program_id(1)

    @pl.when(j == 0)
    def _():
        h = _norm_mod(x_ref[...], g_ref[...], sh_ref[0], sc_ref[0])
        h_ref[...] = h.astype(BF16)

    acc = _dot(h_ref[...], w_ref[...])
    tn = acc.shape[1]

    if rope_tiles:
        @pl.when(j < rope_tiles)
        def _():
            tab = tab_ref[0]
            cos = tab[:, :HEAD_COLS]
            sin_lo = tab[:, HEAD_COLS:2 * HEAD_COLS]
            sin_hi = tab[:, 2 * HEAD_COLS:]
            half = HEAD_DIM // 4
            for c in range(tn // HEAD_COLS):
                xs = acc[:, c * HEAD_COLS:(c + 1) * HEAD_COLS]
                up = pltpu.roll(xs, HEAD_COLS - half, 1)
                dn = pltpu.roll(xs, half, 1)
                y = xs * cos + up * sin_lo + dn * sin_hi
                o_ref[:, c * HEAD_COLS:(c + 1) * HEAD_COLS] = y.astype(o_ref.dtype)

        @pl.when((j >= rope_tiles) & (j < gate_tile0))
        def _():
            o_ref[...] = acc.astype(o_ref.dtype)

        @pl.when(j >= gate_tile0)
        def _():
            o_ref[...] = jax.nn.sigmoid(acc).astype(o_ref.dtype)
    else:
        o_ref[...] = acc.astype(o_ref.dtype)


def in_proj(x2d, g, mod3, mod_row0, w_bf, tabs, *, rows_per_mod, col_tile0, n_cols,
            rope_tiles, gate_tile0, tm, tn):
    m, d = x2d.shape
    tiles_per_mod = rows_per_mod // tm
    q_tiles = rope_tiles // 2 if rope_tiles else 1
    ltiles = tabs.shape[1] // tm if rope_tiles else 1

    def mod_map(which):
        return lambda i, j: ((mod_row0 + i // tiles_per_mod) * N_MOD + which, 0, 0)

    if rope_tiles:
        tab_spec = pl.BlockSpec((1, tm, 3 * HEAD_COLS),
                                lambda i, j: (jnp.minimum(j // q_tiles, 1), i % ltiles, 0))
    else:
        tab_spec = pl.BlockSpec((1, 8, 3 * HEAD_COLS), lambda i, j: (0, 0, 0))

    kern = functools.partial(_in_proj_kernel, rope_tiles=rope_tiles, gate_tile0=gate_tile0)
    return pl.pallas_call(
        kern,
        grid=(m // tm, n_cols // tn),
        in_specs=[pl.BlockSpec((tm, d), lambda i, j: (i, 0)),
                  pl.BlockSpec((1, d), lambda i, j: (0, 0)),
                  pl.BlockSpec((1, 1, d), mod_map(0)),
                  pl.BlockSpec((1, 1, d), mod_map(1)),
                  pl.BlockSpec((d, tn), lambda i, j: (0, j + col_tile0)),
                  tab_spec],
        out_specs=pl.BlockSpec((tm, tn), lambda i, j: (i, j)),
        out_shape=jax.ShapeDtypeStruct((m, n_cols), BF16),
        scratch_shapes=[pltpu.VMEM((tm, d), BF16)],
        compiler_params=_params(("arbitrary", "arbitrary")),
        name="in_proj",
    )(x2d, g.reshape(1, d), mod3, mod3, w_bf, tabs)


def rope_tables(seq_len):
    pos = np.arange(seq_len)
    row = pos // GRID_W
    col = pos % GRID_W
    half = HEAD_DIM // 2
    inv_freq = ROPE_BASE ** (-np.arange(0, half, 2, dtype=np.float64) / half)

    def cs(p):
        ang = p[:, None].astype(np.float64) * inv_freq[None, :]
        ang = np.concatenate([ang, ang], axis=-1)
        return np.cos(ang), np.sin(ang)

    cr, sr = cs(row)
    cc, sc = cs(col)
    cos64 = np.concatenate([cr, cc], axis=-1)
    sin64 = np.concatenate([sr, sc], axis=-1)
    lower = (np.arange(HEAD_DIM) % half) < (half // 2)
    sin_lo64 = np.where(lower[None, :], -sin64, 0.0)
    sin_hi64 = np.where(lower[None, :], 0.0, sin64)
    tab = np.concatenate([np.tile(cos64, (1, 2)), np.tile(sin_lo64, (1, 2)),
                          np.tile(sin_hi64, (1, 2))], axis=-1)
    q_scale = HEAD_DIM ** -0.5 * math.log2(math.e)
    return jnp.asarray(np.stack([tab * q_scale, tab]), dtype=F32)


ATTN_KEY_CHUNK = 512
ATTN_LOOKAHEAD = 2


def _attn_kernel(lq1_ref, lk1_ref, lq2_ref, lk2_ref, gs_ref, q_ref, k_ref, kc_ref, v_ref, vc_ref,
                 o_ref, vt_ref):
    n_lat = k_ref.shape[1]

    @pl.when(pl.program_id(2) == 0)
    def _():
        for off in range(0, n_lat, ATTN_KEY_CHUNK):
            vt_ref[:, off:off + ATTN_KEY_CHUNK] = (
                v_ref[0, off:off + ATTN_KEY_CHUNK].astype(F32).T.astype(BF16))
        vt_ref[:, n_lat:] = vc_ref[0].astype(F32).T.astype(BF16)

    lam = (jnp.exp(jnp.sum(lq1_ref[...] * lk1_ref[...], axis=-1, keepdims=True))
           - jnp.exp(jnp.sum(lq2_ref[...] * lk2_ref[...], axis=-1, keepdims=True))
           + LAM_INIT)
    q = q_ref[0]
    lane = lax.broadcasted_iota(jnp.int32, q.shape, 1)
    qms = [jnp.where((lane < HEAD_DIM) == (comp == 0), q, jnp.zeros_like(q)) for comp in range(2)]
    chunks = ([(off, ATTN_KEY_CHUNK) for off in range(0, n_lat, ATTN_KEY_CHUNK)]
              + [(n_lat, kc_ref.shape[1])])
    items = [(off, width, comp) for off, width in chunks for comp in range(2)]
    ahead = 2 * ATTN_LOOKAHEAD
    scores = {}

    def issue(i):
        off, width, comp = items[i]
        keys = k_ref[0, off:off + width] if off < n_lat else kc_ref[0]
        scores[i] = _dot_nt(keys, qms[comp])

    for i in range(min(ahead, len(items))):
        issue(i)
    m = [None, None]
    d = [None, None]
    acc = [None, None]
    for i, (off, width, comp) in enumerate(items):
        if i + ahead < len(items):
            issue(i + ahead)
        s = scores.pop(i)
        vals_t = vt_ref[:, off:off + width]
        col_max = jnp.max(s, axis=0, keepdims=True)
        if m[comp] is None:
            m[comp] = col_max
            p = jnp.exp2(s - col_max)
            d[comp] = jnp.sum(p, axis=0, keepdims=True)
            acc[comp] = _dot(vals_t, p.astype(BF16))
        else:
            m_new = jnp.maximum(m[comp], col_max)
            alpha = jnp.exp2(m[comp] - m_new)
            p = jnp.exp2(s - m_new)
            d[comp] = alpha * d[comp] + jnp.sum(p, axis=0, keepdims=True)
            acc[comp] = alpha * acc[comp] + _dot(vals_t, p.astype(BF16))
            m[comp] = m_new
    o_t = acc[0] * (1.0 / d[0]) - acc[1] * (lam / d[1])
    o = _rms(o_t.T) * gs_ref[...] * (1.0 - LAM_INIT)
    o_ref[0] = o.astype(o_ref.dtype)


def diff_attn(p3, pc3, lam_vecs, g_subln, *, n_heads, tq):
    b, l, _ = p3.shape
    ctx = pc3.shape[1]
    h = n_heads
    vec = pl.BlockSpec((1, HEAD_DIM), lambda bi, hi, qi: (0, 0))
    return pl.pallas_call(
        _attn_kernel,
        grid=(b, h, l // tq),
        in_specs=[vec, vec, vec, vec,
                  pl.BlockSpec((1, HEAD_COLS), lambda bi, hi, qi: (0, 0)),
                  pl.BlockSpec((1, tq, HEAD_COLS), lambda bi, hi, qi: (bi, qi, hi)),
                  pl.BlockSpec((1, l, HEAD_COLS), lambda bi, hi, qi: (bi, 0, h + hi)),
                  pl.BlockSpec((1, ctx, HEAD_COLS), lambda bi, hi, qi: (bi, 0, hi)),
                  pl.BlockSpec((1, l, HEAD_COLS), lambda bi, hi, qi: (bi, 0, 2 * h + hi)),
                  pl.BlockSpec((1, ctx, HEAD_COLS), lambda bi, hi, qi: (bi, 0, h + hi))],
        out_specs=pl.BlockSpec((1, tq, HEAD_COLS), lambda bi, hi, qi: (bi, qi, hi)),
        out_shape=jax.ShapeDtypeStruct((b, l, h * HEAD_COLS), BF16),
        scratch_shapes=[pltpu.VMEM((HEAD_COLS, l + ctx), BF16)],
        compiler_params=_params(("arbitrary", "arbitrary", "arbitrary")),
        name="diff_attn",
    )(*lam_vecs, g_subln.reshape(1, HEAD_COLS), p3, p3, pc3, p3, pc3)


FOURIER_N1 = 16


def _bf16_const(a):
    return jnp.asarray(a, dtype=F32).astype(BF16)


def dft_tables(n, scale):
    idx = (np.arange(n)[:, None] * np.arange(n)[None, :]) % n
    ang = 2.0 * np.pi * idx / n
    return np.cos(ang) * scale, np.sin(ang) * scale


def stage_a_tables(l, n1, sub, scale):
    n2 = l // n1
    j = np.arange(n2 // sub)[:, None, None, None]
    k1 = np.arange(n1)[None, :, None, None]
    a = np.arange(sub)[None, None, :, None]
    m1 = np.arange(n1)[None, None, None, :]
    ang = 2.0 * np.pi * ((k1 * (m1 * n2 + j * sub + a)) % l) / l
    eye = np.eye(sub)
    c = np.einsum("jkam,ab->jkamb", np.cos(ang) * scale, eye).reshape(n2 // sub, n1 * sub, n1 * sub)
    s = np.einsum("jkam,ab->jkamb", np.sin(ang) * scale, eye).reshape(n2 // sub, n1 * sub, n1 * sub)
    return np.concatenate([c, -s], axis=1), np.concatenate([-s, -c], axis=1)


def _fourier_a_kernel(cc_ref, sc_ref, ka_ref, kb_ref, u_ref, yr_ref, yi_ref, *, gd):
    rows = ka_ref.shape[2]
    u = u_ref[0].reshape(rows, u_ref.shape[3])
    groups = range(u.shape[1] // gd)
    uc = jnp.concatenate([_dot(u[:, g * gd:(g + 1) * gd], cc_ref[...]) for g in groups], axis=1)
    us = jnp.concatenate([_dot(u[:, g * gd:(g + 1) * gd], sc_ref[...]) for g in groups], axis=1)
    y = _dot(ka_ref[0], uc.astype(BF16)) + _dot(kb_ref[0], us.astype(BF16))
    yr_ref[0] = y[:rows].astype(yr_ref.dtype).reshape(yr_ref.shape[1:])
    yi_ref[0] = y[rows:].astype(yi_ref.dtype).reshape(yi_ref.shape[1:])


def fourier_a(p4, cos_c, sin_c, ka, kb, *, col_block, width):
    b, n1, n2, _ = p4.shape
    gd = cos_c.shape[0]
    tiles, rows2, rows = ka.shape
    sub = rows // n1
    tab = pl.BlockSpec((1, rows2, rows), lambda bi, j: (j, 0, 0))
    out_spec = pl.BlockSpec((1, n1, sub, width), lambda bi, j: (bi, 0, j, 0))
    out = jax.ShapeDtypeStruct((b, n1, n2, width), BF16)
    return pl.pallas_call(
        functools.partial(_fourier_a_kernel, gd=gd),
        grid=(b, tiles),
        in_specs=[_resident((gd, gd)), _resident((gd, gd)), tab, tab,
                  pl.BlockSpec((1, n1, sub, width), lambda bi, j: (bi, 0, j, col_block))],
        out_specs=[out_spec, out_spec],
        out_shape=[out, out],
        compiler_params=_params(("arbitrary", "arbitrary")),
        name="fourier_a",
    )(cos_c, sin_c, ka, kb, p4)


def _fourier_c_kernel(c_ref, s_ref, yr_ref, yi_ref, o_ref):
    o = _dot(c_ref[...], yr_ref[0, 0]) + _dot(s_ref[...], yi_ref[0, 0])
    o_ref[0, 0] = o.astype(o_ref.dtype)


def fourier_c(yr4, yi4, cos2, sin2):
    b, n1, n2, width = yr4.shape
    data = pl.BlockSpec((1, 1, n2, width), lambda bi, k: (bi, k, 0, 0))
    return pl.pallas_call(
        _fourier_c_kernel,
        grid=(b, n1),
        in_specs=[_resident((n2, n2)), _resident((n2, n2)), data, data],
        out_specs=data,
        out_shape=jax.ShapeDtypeStruct((b, n1, n2, width), BF16),
        compiler_params=_params(("arbitrary", "arbitrary")),
        name="fourier_c",
    )(cos2, sin2, yr4, yi4)


def position_order_matrix(n1, sub):
    p = np.zeros((sub * n1, n1 * sub))
    for k1 in range(n1):
        for k2 in range(sub):
            p[k2 * n1 + k1, k1 * sub + k2] = 1.0
    return p


def _merge_kernel(a_ref, f_ref, ga_ref, gf_ref, x_ref, gt_ref, pm_ref, wa_ref, wf_ref, wo_ref, o_ref):
    ya = _dot(a_ref[...], wa_ref[...])
    f = f_ref[0].reshape(pm_ref.shape[1], f_ref.shape[3])
    yf = _dot(_dot(pm_ref[...], f).astype(BF16), wf_ref[...])
    merged = ga_ref[...].astype(F32) * ya + gf_ref[...].astype(F32) * yf
    y = _dot(merged.astype(BF16), wo_ref[...])
    o_ref[...] = x_ref[...] + gt_ref[0] * y


def merge(attn2d, four4, pm, p2d, x2d, mod3, wa, wf, wo, *, rows_per_mod, gate_col0):
    m, d = x2d.shape
    dw = attn2d.shape[1]
    _, n1, _, fw = four4.shape
    tm = pm.shape[0]
    sub = tm // n1
    tiles_per_mod = rows_per_mod // tm
    ga_blk = gate_col0 // d
    return pl.pallas_call(
        _merge_kernel,
        grid=(m // tm,),
        in_specs=[pl.BlockSpec((tm, dw), lambda i: (i, 0)),
                  pl.BlockSpec((1, n1, sub, fw), lambda i: (i // tiles_per_mod, 0, i % tiles_per_mod, 0)),
                  pl.BlockSpec((tm, d), lambda i: (i, ga_blk)),
                  pl.BlockSpec((tm, d), lambda i: (i, ga_blk + 1)),
                  pl.BlockSpec((tm, d), lambda i: (i, 0)),
                  pl.BlockSpec((1, 1, d), lambda i: ((i // tiles_per_mod) * N_MOD + 2, 0, 0)),
                  _resident(pm.shape), _resident(wa.shape), _resident(wf.shape), _resident(wo.shape)],
        out_specs=pl.BlockSpec((tm, d), lambda i: (i, 0)),
        out_shape=jax.ShapeDtypeStruct((m, d), F32),
        compiler_params=_params(("arbitrary",)),
        name="merge",
    )(attn2d, four4, p2d, p2d, x2d, mod3, pm, wa, wf, wo)


def _mlp_kernel(x_ref, g_ref, sh_ref, sc_ref, gt_ref, gf_ref, w1_ref, w2_ref, o_ref, h_ref, acc_ref):
    j = pl.program_id(1)

    @pl.when(j == 0)
    def _():
        h = _norm_mod(x_ref[...], g_ref[...], sh_ref[0], sc_ref[0])
        h_ref[...] = h.astype(BF16)
        acc_ref[...] = jnp.zeros_like(acc_ref)

    z = jnp.maximum(_dot(h_ref[...], w1_ref[...]), 0.0)
    acc_ref[...] += _dot((z * z).astype(BF16), w2_ref[...])

    @pl.when(j == pl.num_programs(1) - 1)
    def _():
        y = x_ref[...] + gt_ref[0] * acc_ref[...]
        o_ref[...] = _rms(y) * gf_ref[...]


def mlp(x2d, g2, mod3, g_final, w1, w2, *, rows_per_mod, tm, tf):
    m, d = x2d.shape
    f = w1.shape[1]
    tiles_per_mod = rows_per_mod // tm

    def mod_map(which):
        return lambda i, j: ((i // tiles_per_mod) * N_MOD + which, 0, 0)

    return pl.pallas_call(
        _mlp_kernel,
        grid=(m // tm, f // tf),
        in_specs=[pl.BlockSpec((tm, d), lambda i, j: (i, 0)),
                  pl.BlockSpec((1, d), lambda i, j: (0, 0)),
                  pl.BlockSpec((1, 1, d), mod_map(3)),
                  pl.BlockSpec((1, 1, d), mod_map(4)),
                  pl.BlockSpec((1, 1, d), mod_map(5)),
                  pl.BlockSpec((1, d), lambda i, j: (0, 0)),
                  pl.BlockSpec((d, tf), lambda i, j: (0, j)),
                  pl.BlockSpec((tf, d), lambda i, j: (j, 0))],
        out_specs=pl.BlockSpec((tm, d), lambda i, j: (i, 0), pipeline_mode=pl.Buffered(1)),
        out_shape=jax.ShapeDtypeStruct((m, d), F32),
        scratch_shapes=[pltpu.VMEM((tm, d), BF16), pltpu.VMEM((tm, d), F32)],
        compiler_params=_params(("arbitrary", "arbitrary")),
        name="mlp",
    )(x2d, g2.reshape(1, d), mod3, mod3, mod3, g_final.reshape(1, d), w1, w2)


def kernel(x, c, ctx, c_ctx, w_ada, b_ada, g_norm1, w_in, lam_q1, lam_k1, lam_q2, lam_k2, g_subln,
           w_attn_br, w_four_br, w_out, g_norm2, w_mlp_in, w_mlp_out, g_final):
    b, l, d = x.shape
    n_ctx = ctx.shape[1]
    assert w_ada.shape[0] == 1, "single-layer block"
    dw = w_attn_br.shape[1]
    fw = w_four_br.shape[1]
    n_heads = dw // HEAD_COLS
    gd = fw // N_FOURIER_GROUPS
    in_cols = w_in.shape[2]
    gate_col0 = 3 * dw + fw
    assert in_cols == gate_col0 + 2 * d

    cvec = jnp.zeros((8, d), F32).at[:b].set(c).at[b].set(c_ctx)
    mod = ada_mod(cvec, w_ada[0], b_ada[0])
    mod3 = mod[:b + 1].reshape((b + 1) * N_MOD, 1, d)

    w_in_bf = w_in[0].astype(BF16)
    tabs = rope_tables(l)
    tm, tn = 1024, 1024
    x2d = x.reshape(b * l, d)
    p2d = in_proj(x2d, g_norm1[0], mod3, 0, w_in_bf, tabs, rows_per_mod=l, col_tile0=0,
                  n_cols=in_cols, rope_tiles=2 * dw // tn, gate_tile0=gate_col0 // tn, tm=tm, tn=tn)
    pc2d = in_proj(ctx.reshape(b * n_ctx, d), g_norm1[0], mod3, b, w_in_bf, tabs,
                   rows_per_mod=b * n_ctx, col_tile0=dw // tn, n_cols=2 * dw,
                   rope_tiles=0, gate_tile0=0, tm=b * n_ctx, tn=tn)

    lam_vecs = [v[0].astype(F32).reshape(1, HEAD_DIM) for v in (lam_q1, lam_k1, lam_q2, lam_k2)]
    attn = diff_attn(p2d.reshape(b, l, in_cols), pc2d.reshape(b, n_ctx, 2 * dw), lam_vecs,
                     g_subln[0], n_heads=n_heads, tq=512)

    n1 = FOURIER_N1
    n2 = l // n1
    sub = FOURIER_N1
    assert n1 * n2 == l and n2 % sub == 0
    cos_c, sin_c = (_bf16_const(t) for t in dft_tables(gd, gd ** -0.5))
    ka, kb = (_bf16_const(t) for t in stage_a_tables(l, n1, sub, l ** -0.5))
    cos2, sin2 = (_bf16_const(t) for t in dft_tables(n2, 1.0))
    yr, yi = fourier_a(p2d.reshape(b, n1, n2, in_cols), cos_c, sin_c, ka, kb,
                       col_block=3 * dw // fw, width=fw)
    four4 = fourier_c(yr, yi, cos2, sin2)

    x_mid = merge(attn.reshape(b * l, dw), four4, _bf16_const(position_order_matrix(n1, sub)), p2d, x2d,
                  mod3, w_attn_br[0].astype(BF16), w_four_br[0].astype(BF16), w_out[0].astype(BF16),
                  rows_per_mod=l, gate_col0=gate_col0)
    out = mlp(x_mid, g_norm2[0], mod3, g_final, w_mlp_in[0].astype(BF16), w_mlp_out[0].astype(BF16),
              rows_per_mod=l, tm=1024, tf=512)
    return out.reshape(b, l, d)
```

```python
import functools
import math

import jax
import jax.numpy as jnp
import numpy as np
from jax import lax
from jax.experimental import pallas as pl
from jax.experimental.pallas import tpu as pltpu

GRID_W = 64
HEAD_DIM = 64
HEAD_COLS = 2 * HEAD_DIM
N_FOURIER_GROUPS = 4
N_MOD = 6
ROPE_BASE = 10000.0
EPS = 1e-6
LAM_INIT = 0.8 - 0.6 * math.exp(-0.3 * 0)

V7X_VMEM_LIMIT_BYTES = 58 * 1024 * 1024
BF16 = jnp.bfloat16
BF16_ROW_TILE = 16
F32 = jnp.float32


def _params(semantics, flags=None):
    return pltpu.CompilerParams(dimension_semantics=semantics,
                                vmem_limit_bytes=V7X_VMEM_LIMIT_BYTES, flags=flags)


def _dot(a, b):
    return jnp.dot(a, b, preferred_element_type=F32)


def _dot_nt(a, b):
    return lax.dot_general(a, b, (((1,), (1,)), ((), ())), preferred_element_type=F32)


def _resident(shape):
    zeros = (0,) * len(shape)
    return pl.BlockSpec(shape, lambda *_: zeros, pipeline_mode=pl.Buffered(1))


def _ada_kernel(c_ref, w_ref, b_ref, o_ref):
    cv = c_ref[...]
    o_ref[...] = _dot(cv * jax.nn.sigmoid(cv), w_ref[...]) + b_ref[...]


def ada_mod(cvec, w_ada, b_ada, *, tn=1024):
    rows, d = cvec.shape
    n = w_ada.shape[1]
    return pl.pallas_call(
        _ada_kernel,
        grid=(n // tn,),
        in_specs=[pl.BlockSpec((rows, d), lambda j: (0, 0)),
                  pl.BlockSpec((d, tn), lambda j: (0, j)),
                  pl.BlockSpec((1, tn), lambda j: (0, j))],
        out_specs=pl.BlockSpec((rows, tn), lambda j: (0, j)),
        out_shape=jax.ShapeDtypeStruct((rows, n), F32),
        compiler_params=_params(("arbitrary",)),
        name="ada_mod",
    )(cvec, w_ada, b_ada.reshape(1, n))


def _rms(x):
    return x * lax.rsqrt(jnp.mean(x * x, axis=-1, keepdims=True) + EPS)


def _norm_mod(x, g, shift, scale):
    return _rms(x) * g * (1.0 + scale) + shift


def _in_proj_kernel(x_ref, g_ref, sh_ref, sc_ref, w_ref, tab_ref, o_ref, h_ref, *,
                    rope_tiles, gate_tile0):
    j = pl.program_id(1)

    @pl.when(j == 0)
    def _():
        h = _norm_mod(x_ref[...], g_ref[...], sh_ref[0], sc_ref[0])
        h_ref[...] = h.astype(BF16)

    acc = _dot(h_ref[...], w_ref[...].astype(BF16))
    tn = acc.shape[1]

    if rope_tiles:
        @pl.when(j < rope_tiles)
        def _():
            tab = tab_ref[0]
            cos = tab[:, :HEAD_COLS]
            sin_lo = tab[:, HEAD_COLS:2 * HEAD_COLS]
            sin_hi = tab[:, 2 * HEAD_COLS:]
            half = HEAD_DIM // 4
            for c in range(tn // HEAD_COLS):
                xs = acc[:, c * HEAD_COLS:(c + 1) * HEAD_COLS]
                up = pltpu.roll(xs, HEAD_COLS - half, 1)
                dn = pltpu.roll(xs, half, 1)
                y = xs * cos + up * sin_lo + dn * sin_hi
                o_ref[:, c * HEAD_COLS:(c + 1) * HEAD_COLS] = y.astype(o_ref.dtype)

        @pl.when((j >= rope_tiles) & (j < gate_tile0))
        def _():
            o_ref[...] = acc.astype(o_ref.dtype)

        @pl.when(j >= gate_tile0)
        def _():
            o_ref[...] = jax.nn.sigmoid(acc).astype(o_ref.dtype)
    else:
        o_ref[...] = acc.astype(o_ref.dtype)


def in_proj(x2d, g, mod3, mod_row0, w, tabs, *, rows_per_mod, col_tile0, n_cols,
            rope_tiles, gate_tile0, tm, tn):
    m, d = x2d.shape
    tiles_per_mod = rows_per_mod // tm
    q_tiles = rope_tiles // 2 if rope_tiles else 1
    ltiles = tabs.shape[1] // tm if rope_tiles else 1

    def mod_map(which):
        return lambda i, j: ((mod_row0 + i // tiles_per_mod) * N_MOD + which, 0, 0)

    if rope_tiles:
        tab_spec = pl.BlockSpec((1, tm, 3 * HEAD_COLS),
                                lambda i, j: (jnp.minimum(j // q_tiles, 1), i % ltiles, 0))
    else:
        tab_spec = pl.BlockSpec((1, 8, 3 * HEAD_COLS), lambda i, j: (0, 0, 0))

    kern = functools.partial(_in_proj_kernel, rope_tiles=rope_tiles, gate_tile0=gate_tile0)
    return pl.pallas_call(
        kern,
        grid=(m // tm, n_cols // tn),
        in_specs=[pl.BlockSpec((tm, d), lambda i, j: (i, 0)),
                  pl.BlockSpec((1, d), lambda i, j: (0, 0)),
                  pl.BlockSpec((1, 1, d), mod_map(0)),
                  pl.BlockSpec((1, 1, d), mod_map(1)),
                  pl.BlockSpec((d, tn), lambda i, j: (0, j + col_tile0)),
                  tab_spec],
        out_specs=pl.BlockSpec((tm, tn), lambda i, j: (i, j)),
        out_shape=jax.ShapeDtypeStruct((m, n_cols), BF16),
        scratch_shapes=[pltpu.VMEM((tm, d), BF16)],
        compiler_params=_params(("arbitrary", "arbitrary")),
        name="in_proj",
    )(x2d, g.reshape(1, d), mod3, mod3, w, tabs)


def rope_tables(seq_len):
    pos = np.arange(seq_len)
    row = pos // GRID_W
    col = pos % GRID_W
    half = HEAD_DIM // 2
    inv_freq = ROPE_BASE ** (-np.arange(0, half, 2, dtype=np.float64) / half)

    def cs(p):
        ang = p[:, None].astype(np.float64) * inv_freq[None, :]
        ang = np.concatenate([ang, ang], axis=-1)
        return np.cos(ang), np.sin(ang)

    cr, sr = cs(row)
    cc, sc = cs(col)
    cos64 = np.concatenate([cr, cc], axis=-1)
    sin64 = np.concatenate([sr, sc], axis=-1)
    lower = (np.arange(HEAD_DIM) % half) < (half // 2)
    sin_lo64 = np.where(lower[None, :], -sin64, 0.0)
    sin_hi64 = np.where(lower[None, :], 0.0, sin64)
    tab = np.concatenate([np.tile(cos64, (1, 2)), np.tile(sin_lo64, (1, 2)),
                          np.tile(sin_hi64, (1, 2))], axis=-1)
    q_scale = HEAD_DIM ** -0.5 * math.log2(math.e)
    return jnp.asarray(np.stack([tab * q_scale, tab]), dtype=F32)


ATTN_KEY_CHUNK = 512
ATTN_LOOKAHEAD = 2


def _attn_kernel(lq1_ref, lk1_ref, lq2_ref, lk2_ref, gs_ref, q_ref, k_ref, kc_ref, v_ref, vc_ref,
                 *rest, n_cast):
    w_refs, o_ref, wbf_refs, vt_ref = rest[:n_cast], rest[n_cast], rest[n_cast + 1:-1], rest[-1]
    n_lat = k_ref.shape[1]

    for w_ref, wbf_ref in zip(w_refs, wbf_refs):
        wbf_ref[...] = w_ref[...].astype(wbf_ref.dtype)

    @pl.when(pl.program_id(2) == 0)
    def _():
        for off in range(0, n_lat, ATTN_KEY_CHUNK):
            vt_ref[:, off:off + ATTN_KEY_CHUNK] = (
                v_ref[0, off:off + ATTN_KEY_CHUNK].astype(F32).T.astype(BF16))
        vt_ref[:, n_lat:] = vc_ref[0].astype(F32).T.astype(BF16)

    lam = (jnp.exp(jnp.sum(lq1_ref[...] * lk1_ref[...], axis=-1, keepdims=True))
           - jnp.exp(jnp.sum(lq2_ref[...] * lk2_ref[...], axis=-1, keepdims=True))
           + LAM_INIT)
    q = q_ref[0]
    lane = lax.broadcasted_iota(jnp.int32, q.shape, 1)
    qms = [jnp.where((lane < HEAD_DIM) == (comp == 0), q, jnp.zeros_like(q)) for comp in range(2)]
    chunks = ([(off, ATTN_KEY_CHUNK) for off in range(0, n_lat, ATTN_KEY_CHUNK)]
              + [(n_lat, kc_ref.shape[1])])
    items = [(off, width, comp) for off, width in chunks for comp in range(2)]
    ahead = 2 * ATTN_LOOKAHEAD
    scores = {}

    def issue(i):
        off, width, comp = items[i]
        keys = k_ref[0, off:off + width] if off < n_lat else kc_ref[0]
        scores[i] = _dot_nt(keys, qms[comp])

    for i in range(min(ahead, len(items))):
        issue(i)
    m = [None, None]
    d = [None, None]
    acc = [None, None]
    for i, (off, width, comp) in enumerate(items):
        if i + ahead < len(items):
            issue(i + ahead)
        s = scores.pop(i)
        vals_t = vt_ref[:, off:off + width]
        col_max = jnp.max(s, axis=0, keepdims=True)
        if m[comp] is None:
            m[comp] = col_max
            p = jnp.exp2(s - col_max)
            d[comp] = jnp.sum(p, axis=0, keepdims=True)
            acc[comp] = _dot(vals_t, p.astype(BF16))
        else:
            m_new = jnp.maximum(m[comp], col_max)
            alpha = jnp.exp2(m[comp] - m_new)
            p = jnp.exp2(s - m_new)
            d[comp] = alpha * d[comp] + jnp.sum(p, axis=0, keepdims=True)
            acc[comp] = alpha * acc[comp] + _dot(vals_t, p.astype(BF16))
            m[comp] = m_new
    o_t = acc[0] * (1.0 / d[0]) - acc[1] * (lam / d[1])
    o = _rms(o_t.T) * gs_ref[...] * (1.0 - LAM_INIT)
    o_ref[0] = o.astype(o_ref.dtype)


def diff_attn(p3, pc3, lam_vecs, g_subln, cast_weights, *, n_heads, tq):
    b, l, _ = p3.shape
    ctx = pc3.shape[1]
    h = n_heads
    nq = l // tq
    n_steps = b * h * nq
    vec = pl.BlockSpec((1, HEAD_DIM), lambda bi, hi, qi: (0, 0))

    def slab_spec(w):
        rows, cols = w.shape
        blk = max(BF16_ROW_TILE, rows // n_steps)
        last = rows // blk - 1
        return pl.BlockSpec(
            (blk, cols), lambda bi, hi, qi: (jnp.minimum((bi * h + hi) * nq + qi, last), 0))

    slabs = [slab_spec(w) for w in cast_weights]
    outs = pl.pallas_call(
        functools.partial(_attn_kernel, n_cast=len(cast_weights)),
        grid=(b, h, nq),
        in_specs=[vec, vec, vec, vec,
                  pl.BlockSpec((1, HEAD_COLS), lambda bi, hi, qi: (0, 0)),
                  pl.BlockSpec((1, tq, HEAD_COLS), lambda bi, hi, qi: (bi, qi, hi)),
                  pl.BlockSpec((1, l, HEAD_COLS), lambda bi, hi, qi: (bi, 0, h + hi)),
                  pl.BlockSpec((1, ctx, HEAD_COLS), lambda bi, hi, qi: (bi, 0, hi)),
                  pl.BlockSpec((1, l, HEAD_COLS), lambda bi, hi, qi: (bi, 0, 2 * h + hi)),
                  pl.BlockSpec((1, ctx, HEAD_COLS), lambda bi, hi, qi: (bi, 0, h + hi))] + slabs,
        out_specs=[pl.BlockSpec((1, tq, HEAD_COLS), lambda bi, hi, qi: (bi, qi, hi))] + slabs,
        out_shape=[jax.ShapeDtypeStruct((b, l, h * HEAD_COLS), BF16)]
                  + [jax.ShapeDtypeStruct(w.shape, BF16) for w in cast_weights],
        scratch_shapes=[pltpu.VMEM((HEAD_COLS, l + ctx), BF16)],
        compiler_params=_params(("arbitrary", "arbitrary", "arbitrary")),
        name="diff_attn",
    )(*lam_vecs, g_subln.reshape(1, HEAD_COLS), p3, p3, pc3, p3, pc3, *cast_weights)
    return outs[0], outs[1:]


FOURIER_N1 = 16


def _bf16_const(a):
    return jnp.asarray(a, dtype=F32).astype(BF16)


def dft_tables(n, scale):
    idx = (np.arange(n)[:, None] * np.arange(n)[None, :]) % n
    ang = 2.0 * np.pi * idx / n
    return np.cos(ang) * scale, np.sin(ang) * scale


def stage_a_tables(l, n1, sub, scale):
    n2 = l // n1
    j = np.arange(n2 // sub)[:, None, None, None]
    k1 = np.arange(n1)[None, :, None, None]
    a = np.arange(sub)[None, None, :, None]
    m1 = np.arange(n1)[None, None, None, :]
    ang = 2.0 * np.pi * ((k1 * (m1 * n2 + j * sub + a)) % l) / l
    eye = np.eye(sub)
    c = np.einsum("jkam,ab->jkamb", np.cos(ang) * scale, eye).reshape(n2 // sub, n1 * sub, n1 * sub)
    s = np.einsum("jkam,ab->jkamb", np.sin(ang) * scale, eye).reshape(n2 // sub, n1 * sub, n1 * sub)
    return np.concatenate([c, -s], axis=1), np.concatenate([-s, -c], axis=1)


def _fourier_a_kernel(cc_ref, sc_ref, ka_ref, kb_ref, u_ref, yr_ref, yi_ref, *, gd):
    rows = ka_ref.shape[2]
    u = u_ref[0].reshape(rows, u_ref.shape[3])
    groups = range(u.shape[1] // gd)
    uc = jnp.concatenate([_dot(u[:, g * gd:(g + 1) * gd], cc_ref[...]) for g in groups], axis=1)
    us = jnp.concatenate([_dot(u[:, g * gd:(g + 1) * gd], sc_ref[...]) for g in groups], axis=1)
    y = _dot(ka_ref[0], uc.astype(BF16)) + _dot(kb_ref[0], us.astype(BF16))
    yr_ref[0] = y[:rows].astype(yr_ref.dtype).reshape(yr_ref.shape[1:])
    yi_ref[0] = y[rows:].astype(yi_ref.dtype).reshape(yi_ref.shape[1:])


def fourier_a(p4, cos_c, sin_c, ka, kb, *, col_block, width):
    b, n1, n2, _ = p4.shape
    gd = cos_c.shape[0]
    tiles, rows2, rows = ka.shape
    sub = rows // n1
    tab = pl.BlockSpec((1, rows2, rows), lambda bi, j: (j, 0, 0))
    out_spec = pl.BlockSpec((1, n1, sub, width), lambda bi, j: (bi, 0, j, 0))
    out = jax.ShapeDtypeStruct((b, n1, n2, width), BF16)
    return pl.pallas_call(
        functools.partial(_fourier_a_kernel, gd=gd),
        grid=(b, tiles),
        in_specs=[_resident((gd, gd)), _resident((gd, gd)), tab, tab,
                  pl.BlockSpec((1, n1, sub, width), lambda bi, j: (bi, 0, j, col_block))],
        out_specs=[out_spec, out_spec],
        out_shape=[out, out],
        compiler_params=_params(("arbitrary", "arbitrary")),
        name="fourier_a",
    )(cos_c, sin_c, ka, kb, p4)


def _fourier_c_kernel(c_ref, s_ref, yr_ref, yi_ref, o_ref):
    o = _dot(c_ref[...], yr_ref[0, 0]) + _dot(s_ref[...], yi_ref[0, 0])
    o_ref[0, 0] = o.astype(o_ref.dtype)


def fourier_c(yr4, yi4, cos2, sin2):
    b, n1, n2, width = yr4.shape
    data = pl.BlockSpec((1, 1, n2, width), lambda bi, k: (bi, k, 0, 0))
    return pl.pallas_call(
        _fourier_c_kernel,
        grid=(b, n1),
        in_specs=[_resident((n2, n2)), _resident((n2, n2)), data, data],
        out_specs=data,
        out_shape=jax.ShapeDtypeStruct((b, n1, n2, width), BF16),
        compiler_params=_params(("arbitrary", "arbitrary")),
        name="fourier_c",
    )(cos2, sin2, yr4, yi4)


def position_order_matrix(n1, sub):
    p = np.zeros((sub * n1, n1 * sub))
    for k1 in range(n1):
        for k2 in range(sub):
            p[k2 * n1 + k1, k1 * sub + k2] = 1.0
    return p


def _merge_kernel(a_ref, f_ref, ga_ref, gf_ref, x_ref, gt_ref, pm_ref, wa_ref, wf_ref, wo_ref, o_ref):
    ya = _dot(a_ref[...], wa_ref[...])
    f = f_ref[0].reshape(pm_ref.shape[1], f_ref.shape[3])
    yf = _dot(_dot(pm_ref[...], f).astype(BF16), wf_ref[...])
    merged = ga_ref[...].astype(F32) * ya + gf_ref[...].astype(F32) * yf
    y = _dot(merged.astype(BF16), wo_ref[...])
    o_ref[...] = x_ref[...] + gt_ref[0] * y


def merge(attn2d, four4, pm, p2d, x2d, mod3, wa, wf, wo, *, rows_per_mod, gate_col0):
    m, d = x2d.shape
    dw = attn2d.shape[1]
    _, n1, _, fw = four4.shape
    tm = pm.shape[0]
    sub = tm // n1
    tiles_per_mod = rows_per_mod // tm
    ga_blk = gate_col0 // d
    return pl.pallas_call(
        _merge_kernel,
        grid=(m // tm,),
        in_specs=[pl.BlockSpec((tm, dw), lambda i: (i, 0)),
                  pl.BlockSpec((1, n1, sub, fw), lambda i: (i // tiles_per_mod, 0, i % tiles_per_mod, 0)),
                  pl.BlockSpec((tm, d), lambda i: (i, ga_blk)),
                  pl.BlockSpec((tm, d), lambda i: (i, ga_blk + 1)),
                  pl.BlockSpec((tm, d), lambda i: (i, 0)),
                  pl.BlockSpec((1, 1, d), lambda i: ((i // tiles_per_mod) * N_MOD + 2, 0, 0)),
                  _resident(pm.shape), _resident(wa.shape), _resident(wf.shape), _resident(wo.shape)],
        out_specs=pl.BlockSpec((tm, d), lambda i: (i, 0)),
        out_shape=jax.ShapeDtypeStruct((m, d), F32),
        compiler_params=_params(("arbitrary",)),
        name="merge",
    )(attn2d, four4, p2d, p2d, x2d, mod3, pm, wa, wf, wo)


def _mlp_kernel(x_ref, g_ref, sh_ref, sc_ref, gt_ref, gf_ref, w1_ref, w2_ref, o_ref, h_ref, acc_ref):
    j = pl.program_id(1)

    @pl.when(j == 0)
    def _():
        h = _norm_mod(x_ref[...], g_ref[...], sh_ref[0], sc_ref[0])
        h_ref[...] = h.astype(BF16)
        acc_ref[...] = jnp.zeros_like(acc_ref)

    z = jnp.maximum(_dot(h_ref[...], w1_ref[...]), 0.0)
    acc_ref[...] += _dot((z * z).astype(BF16), w2_ref[...])

    @pl.when(j == pl.num_programs(1) - 1)
    def _():
        y = x_ref[...] + gt_ref[0] * acc_ref[...]
        o_ref[...] = _rms(y) * gf_ref[...]


def mlp(x2d, g2, mod3, g_final, w1, w2, *, rows_per_mod, tm, tf):
    m, d = x2d.shape
    f = w1.shape[1]
    tiles_per_mod = rows_per_mod // tm

    def mod_map(which):
        return lambda i, j: ((i // tiles_per_mod) * N_MOD + which, 0, 0)

    return pl.pallas_call(
        _mlp_kernel,
        grid=(m // tm, f // tf),
        in_specs=[pl.BlockSpec((tm, d), lambda i, j: (i, 0)),
                  pl.BlockSpec((1, d), lambda i, j: (0, 0)),
                  pl.BlockSpec((1, 1, d), mod_map(3)),
                  pl.BlockSpec((1, 1, d), mod_map(4)),
                  pl.BlockSpec((1, 1, d), mod_map(5)),
                  pl.BlockSpec((1, d), lambda i, j: (0, 0)),
                  pl.BlockSpec((d, tf), lambda i, j: (0, j)),
                  pl.BlockSpec((tf, d), lambda i, j: (j, 0))],
        out_specs=pl.BlockSpec((tm, d), lambda i, j: (i, 0), pipeline_mode=pl.Buffered(1)),
        out_shape=jax.ShapeDtypeStruct((m, d), F32),
        scratch_shapes=[pltpu.VMEM((tm, d), BF16), pltpu.VMEM((tm, d), F32)],
        compiler_params=_params(("arbitrary", "arbitrary")),
        name="mlp",
    )(x2d, g2.reshape(1, d), mod3, mod3, mod3, g_final.reshape(1, d), w1, w2)


def kernel(x, c, ctx, c_ctx, w_ada, b_ada, g_norm1, w_in, lam_q1, lam_k1, lam_q2, lam_k2, g_subln,
           w_attn_br, w_four_br, w_out, g_norm2, w_mlp_in, w_mlp_out, g_final):
    b, l, d = x.shape
    n_ctx = ctx.shape[1]
    assert w_ada.shape[0] == 1, "single-layer block"
    dw = w_attn_br.shape[1]
    fw = w_four_br.shape[1]
    n_heads = dw // HEAD_COLS
    gd = fw // N_FOURIER_GROUPS
    in_cols = w_in.shape[2]
    gate_col0 = 3 * dw + fw
    assert in_cols == gate_col0 + 2 * d

    cvec = jnp.zeros((8, d), F32).at[:b].set(c).at[b].set(c_ctx)
    mod = ada_mod(cvec, w_ada[0], b_ada[0])
    mod3 = mod[:b + 1].reshape((b + 1) * N_MOD, 1, d)

    tabs = rope_tables(l)
    tm, tn = 1024, 1024
    x2d = x.reshape(b * l, d)
    p2d = in_proj(x2d, g_norm1[0], mod3, 0, w_in[0], tabs, rows_per_mod=l, col_tile0=0,
                  n_cols=in_cols, rope_tiles=2 * dw // tn, gate_tile0=gate_col0 // tn, tm=tm, tn=tn)
    pc2d = in_proj(ctx.reshape(b * n_ctx, d), g_norm1[0], mod3, b, w_in[0], tabs,
                   rows_per_mod=b * n_ctx, col_tile0=dw // tn, n_cols=2 * dw,
                   rope_tiles=0, gate_tile0=0, tm=b * n_ctx, tn=tn)

    lam_vecs = [v[0].astype(F32).reshape(1, HEAD_DIM) for v in (lam_q1, lam_k1, lam_q2, lam_k2)]
    attn, (wa_bf, wf_bf, wo_bf, w1_bf, w2_bf) = diff_attn(
        p2d.reshape(b, l, in_cols), pc2d.reshape(b, n_ctx, 2 * dw), lam_vecs, g_subln[0],
        [w_attn_br[0], w_four_br[0], w_out[0], w_mlp_in[0], w_mlp_out[0]], n_heads=n_heads, tq=512)

    n1 = FOURIER_N1
    n2 = l // n1
    sub = FOURIER_N1
    assert n1 * n2 == l and n2 % sub == 0
    cos_c, sin_c = (_bf16_const(t) for t in dft_tables(gd, gd ** -0.5))
    ka, kb = (_bf16_const(t) for t in stage_a_tables(l, n1, sub, l ** -0.5))
    cos2, sin2 = (_bf16_const(t) for t in dft_tables(n2, 1.0))
    yr, yi = fourier_a(p2d.reshape(b, n1, n2, in_cols), cos_c, sin_c, ka, kb,
                       col_block=3 * dw // fw, width=fw)
    four4 = fourier_c(yr, yi, cos2, sin2)

    x_mid = merge(attn.reshape(b * l, dw), four4, _bf16_const(position_order_matrix(n1, sub)), p2d, x2d,
                  mod3, wa_bf, wf_bf, wo_bf, rows_per_mod=l, gate_col0=gate_col0)
    out = mlp(x_mid, g_norm2[0], mod3, g_final, w1_bf, w2_bf,
              rows_per_mod=l, tm=1024, tf=512)
    return out.reshape(b, l, d)
```

```python
import functools
import math

import jax
import jax.numpy as jnp
import numpy as np
from jax import lax
from jax.experimental import pallas as pl
from jax.experimental.pallas import tpu as pltpu

GRID_W = 64
HEAD_DIM = 64
HEAD_COLS = 2 * HEAD_DIM
N_FOURIER_GROUPS = 4
N_MOD = 6
ROPE_BASE = 10000.0
EPS = 1e-6
LAM_INIT = 0.8 - 0.6 * math.exp(-0.3 * 0)

V7X_VMEM_LIMIT_BYTES = 58 * 1024 * 1024
BF16 = jnp.bfloat16
BF16_ROW_TILE = 16
F32 = jnp.float32


def _params(semantics, flags=None):
    return pltpu.CompilerParams(dimension_semantics=semantics,
                                vmem_limit_bytes=V7X_VMEM_LIMIT_BYTES, flags=flags)


def _dot(a, b):
    return jnp.dot(a, b, preferred_element_type=F32)


def _dot_nt(a, b):
    return lax.dot_general(a, b, (((1,), (1,)), ((), ())), preferred_element_type=F32)


def _resident(shape):
    zeros = (0,) * len(shape)
    return pl.BlockSpec(shape, lambda *_: zeros, pipeline_mode=pl.Buffered(1))


def _ada_kernel(c_ref, w_ref, b_ref, o_ref):
    cv = c_ref[...]
    o_ref[...] = _dot(cv * jax.nn.sigmoid(cv), w_ref[...]) + b_ref[...]


def ada_mod(cvec, w_ada, b_ada, *, tn=1024):
    rows, d = cvec.shape
    n = w_ada.shape[1]
    return pl.pallas_call(
        _ada_kernel,
        grid=(n // tn,),
        in_specs=[pl.BlockSpec((rows, d), lambda j: (0, 0)),
                  pl.BlockSpec((d, tn), lambda j: (0, j)),
                  pl.BlockSpec((1, tn), lambda j: (0, j))],
        out_specs=pl.BlockSpec((rows, tn), lambda j: (0, j)),
        out_shape=jax.ShapeDtypeStruct((rows, n), F32),
        compiler_params=_params(("arbitrary",)),
        name="ada_mod",
    )(cvec, w_ada, b_ada.reshape(1, n))


def _rms(x):
    return x * lax.rsqrt(jnp.mean(x * x, axis=-1, keepdims=True) + EPS)


def _norm_mod(x, g, shift, scale):
    return _rms(x) * g * (1.0 + scale) + shift


ROW_SUB = 256


def _rope(xs, cos, sin_lo, sin_hi):
    half = HEAD_DIM // 4
    up = pltpu.roll(xs, HEAD_COLS - half, 1)
    dn = pltpu.roll(xs, half, 1)
    return xs * cos + up * sin_lo + dn * sin_hi


def _in_proj_kernel(x_ref, g_ref, sh_ref, sc_ref, w_ref, tab_ref, o_ref, h_ref, *,
                    rope_tiles, gate_tile0):
    j = pl.program_id(1)
    tm, tn = o_ref.shape
    n_sub = tm // ROW_SUB

    def run(first, kind):
        w = w_ref[...].astype(BF16)
        accs = {}

        def issue(idx):
            rows = slice(idx * ROW_SUB, (idx + 1) * ROW_SUB)
            if first:
                h = _norm_mod(x_ref[rows], g_ref[...], sh_ref[0], sc_ref[0]).astype(BF16)
                h_ref[rows] = h
            else:
                h = h_ref[rows]
            accs[idx] = _dot(h, w)

        issue(0)
        for idx in range(n_sub):
            if idx + 1 < n_sub:
                issue(idx + 1)
            acc = accs.pop(idx)
            rows = slice(idx * ROW_SUB, (idx + 1) * ROW_SUB)
            if kind == "rope":
                tab = tab_ref[0, rows]
                cos = tab[:, :HEAD_COLS]
                sin_lo = tab[:, HEAD_COLS:2 * HEAD_COLS]
                sin_hi = tab[:, 2 * HEAD_COLS:]
                for c in range(tn // HEAD_COLS):
                    cols = slice(c * HEAD_COLS, (c + 1) * HEAD_COLS)
                    o_ref[rows, cols] = _rope(acc[:, cols], cos, sin_lo, sin_hi).astype(o_ref.dtype)
            elif kind == "gate":
                o_ref[rows] = jax.nn.sigmoid(acc).astype(o_ref.dtype)
            else:
                o_ref[rows] = acc.astype(o_ref.dtype)

    if rope_tiles:
        pl.when(j == 0)(functools.partial(run, True, "rope"))
        pl.when((j > 0) & (j < rope_tiles))(functools.partial(run, False, "rope"))
        pl.when((j >= rope_tiles) & (j < gate_tile0))(functools.partial(run, False, "plain"))
        pl.when(j >= gate_tile0)(functools.partial(run, False, "gate"))
    else:
        pl.when(j == 0)(functools.partial(run, True, "plain"))
        pl.when(j > 0)(functools.partial(run, False, "plain"))


def in_proj(x2d, g, mod3, mod_row0, w, tabs, *, rows_per_mod, col_tile0, n_cols,
            rope_tiles, gate_tile0, tm, tn):
    m, d = x2d.shape
    tiles_per_mod = rows_per_mod // tm
    q_tiles = rope_tiles // 2 if rope_tiles else 1
    ltiles = tabs.shape[1] // tm if rope_tiles else 1

    def mod_map(which):
        return lambda i, j: ((mod_row0 + i // tiles_per_mod) * N_MOD + which, 0, 0)

    if rope_tiles:
        tab_spec = pl.BlockSpec((1, tm, 3 * HEAD_COLS),
                                lambda i, j: (jnp.minimum(j // q_tiles, 1), i % ltiles, 0))
    else:
        tab_spec = pl.BlockSpec((1, 8, 3 * HEAD_COLS), lambda i, j: (0, 0, 0))

    kern = functools.partial(_in_proj_kernel, rope_tiles=rope_tiles, gate_tile0=gate_tile0)
    return pl.pallas_call(
        kern,
        grid=(m // tm, n_cols // tn),
        in_specs=[pl.BlockSpec((tm, d), lambda i, j: (i, 0)),
                  pl.BlockSpec((1, d), lambda i, j: (0, 0)),
                  pl.BlockSpec((1, 1, d), mod_map(0)),
                  pl.BlockSpec((1, 1, d), mod_map(1)),
                  pl.BlockSpec((d, tn), lambda i, j: (0, j + col_tile0)),
                  tab_spec],
        out_specs=pl.BlockSpec((tm, tn), lambda i, j: (i, j)),
        out_shape=jax.ShapeDtypeStruct((m, n_cols), BF16),
        scratch_shapes=[pltpu.VMEM((tm, d), BF16)],
        compiler_params=_params(("arbitrary", "arbitrary")),
        name="in_proj",
    )(x2d, g.reshape(1, d), mod3, mod3, w, tabs)


def rope_tables(seq_len):
    pos = np.arange(seq_len)
    row = pos // GRID_W
    col = pos % GRID_W
    half = HEAD_DIM // 2
    inv_freq = ROPE_BASE ** (-np.arange(0, half, 2, dtype=np.float64) / half)

    def cs(p):
        ang = p[:, None].astype(np.float64) * inv_freq[None, :]
        ang = np.concatenate([ang, ang], axis=-1)
        return np.cos(ang), np.sin(ang)

    cr, sr = cs(row)
    cc, sc = cs(col)
    cos64 = np.concatenate([cr, cc], axis=-1)
    sin64 = np.concatenate([sr, sc], axis=-1)
    lower = (np.arange(HEAD_DIM) % half) < (half // 2)
    sin_lo64 = np.where(lower[None, :], -sin64, 0.0)
    sin_hi64 = np.where(lower[None, :], 0.0, sin64)
    tab = np.concatenate([np.tile(cos64, (1, 2)), np.tile(sin_lo64, (1, 2)),
                          np.tile(sin_hi64, (1, 2))], axis=-1)
    q_scale = HEAD_DIM ** -0.5 * math.log2(math.e)
    return jnp.asarray(np.stack([tab * q_scale, tab]), dtype=F32)


ATTN_KEY_CHUNK = 512
ATTN_LOOKAHEAD = 2


def _attn_kernel(lq1_ref, lk1_ref, lq2_ref, lk2_ref, gs_ref, q_ref, k_ref, kc_ref, v_ref, vc_ref,
                 *rest, n_cast):
    w_refs, o_ref, wbf_refs, vt_ref = rest[:n_cast], rest[n_cast], rest[n_cast + 1:-1], rest[-1]
    n_lat = k_ref.shape[1]

    for w_ref, wbf_ref in zip(w_refs, wbf_refs):
        wbf_ref[...] = w_ref[...].astype(wbf_ref.dtype)

    @pl.when(pl.program_id(2) == 0)
    def _():
        for off in range(0, n_lat, ATTN_KEY_CHUNK):
            vt_ref[:, off:off + ATTN_KEY_CHUNK] = (
                v_ref[0, off:off + ATTN_KEY_CHUNK].astype(F32).T.astype(BF16))
        vt_ref[:, n_lat:] = vc_ref[0].astype(F32).T.astype(BF16)

    lam = (jnp.exp(jnp.sum(lq1_ref[...] * lk1_ref[...], axis=-1, keepdims=True))
           - jnp.exp(jnp.sum(lq2_ref[...] * lk2_ref[...], axis=-1, keepdims=True))
           + LAM_INIT)
    q = q_ref[0]
    lane = lax.broadcasted_iota(jnp.int32, q.shape, 1)
    qms = [jnp.where((lane < HEAD_DIM) == (comp == 0), q, jnp.zeros_like(q)) for comp in range(2)]
    chunks = ([(off, ATTN_KEY_CHUNK) for off in range(0, n_lat, ATTN_KEY_CHUNK)]
              + [(n_lat, kc_ref.shape[1])])
    items = [(off, width, comp) for off, width in chunks for comp in range(2)]
    ahead = 2 * ATTN_LOOKAHEAD
    scores = {}

    def issue(i):
        off, width, comp = items[i]
        keys = k_ref[0, off:off + width] if off < n_lat else kc_ref[0]
        scores[i] = _dot_nt(keys, qms[comp])

    for i in range(min(ahead, len(items))):
        issue(i)
    m = [None, None]
    d = [None, None]
    acc = [None, None]
    for i, (off, width, comp) in enumerate(items):
        if i + ahead < len(items):
            issue(i + ahead)
        s = scores.pop(i)
        vals_t = vt_ref[:, off:off + width]
        col_max = jnp.max(s, axis=0, keepdims=True)
        if m[comp] is None:
            m[comp] = col_max
            p = jnp.exp2(s - col_max)
            d[comp] = jnp.sum(p, axis=0, keepdims=True)
            acc[comp] = _dot(vals_t, p.astype(BF16))
        else:
            m_new = jnp.maximum(m[comp], col_max)
            alpha = jnp.exp2(m[comp] - m_new)
            p = jnp.exp2(s - m_new)
            d[comp] = alpha * d[comp] + jnp.sum(p, axis=0, keepdims=True)
            acc[comp] = alpha * acc[comp] + _dot(vals_t, p.astype(BF16))
            m[comp] = m_new
    o_t = acc[0] * (1.0 / d[0]) - acc[1] * (lam / d[1])
    o = _rms(o_t.T) * gs_ref[...] * (1.0 - LAM_INIT)
    o_ref[0] = o.astype(o_ref.dtype)


def diff_attn(p3, pc3, lam_vecs, g_subln, cast_weights, *, n_heads, tq):
    b, l, _ = p3.shape
    ctx = pc3.shape[1]
    h = n_heads
    nq = l // tq
    n_steps = b * h * nq
    vec = pl.BlockSpec((1, HEAD_DIM), lambda bi, hi, qi: (0, 0))

    def slab_spec(w):
        rows, cols = w.shape
        blk = max(BF16_ROW_TILE, rows // n_steps)
        last = rows // blk - 1
        return pl.BlockSpec(
            (blk, cols), lambda bi, hi, qi: (jnp.minimum((bi * h + hi) * nq + qi, last), 0))

    slabs = [slab_spec(w) for w in cast_weights]
    outs = pl.pallas_call(
        functools.partial(_attn_kernel, n_cast=len(cast_weights)),
        grid=(b, h, nq),
        in_specs=[vec, vec, vec, vec,
                  pl.BlockSpec((1, HEAD_COLS), lambda bi, hi, qi: (0, 0)),
                  pl.BlockSpec((1, tq, HEAD_COLS), lambda bi, hi, qi: (bi, qi, hi)),
                  pl.BlockSpec((1, l, HEAD_COLS), lambda bi, hi, qi: (bi, 0, h + hi)),
                  pl.BlockSpec((1, ctx, HEAD_COLS), lambda bi, hi, qi: (bi, 0, hi)),
                  pl.BlockSpec((1, l, HEAD_COLS), lambda bi, hi, qi: (bi, 0, 2 * h + hi)),
                  pl.BlockSpec((1, ctx, HEAD_COLS), lambda bi, hi, qi: (bi, 0, h + hi))] + slabs,
        out_specs=[pl.BlockSpec((1, tq, HEAD_COLS), lambda bi, hi, qi: (bi, qi, hi))] + slabs,
        out_shape=[jax.ShapeDtypeStruct((b, l, h * HEAD_COLS), BF16)]
                  + [jax.ShapeDtypeStruct(w.shape, BF16) for w in cast_weights],
        scratch_shapes=[pltpu.VMEM((HEAD_COLS, l + ctx), BF16)],
        compiler_params=_params(("arbitrary", "arbitrary", "arbitrary")),
        name="diff_attn",
    )(*lam_vecs, g_subln.reshape(1, HEAD_COLS), p3, p3, pc3, p3, pc3, *cast_weights)
    return outs[0], outs[1:]


FOURIER_N1 = 16


def _bf16_const(a):
    return jnp.asarray(a, dtype=F32).astype(BF16)


def dft_tables(n, scale):
    idx = (np.arange(n)[:, None] * np.arange(n)[None, :]) % n
    ang = 2.0 * np.pi * idx / n
    return np.cos(ang) * scale, np.sin(ang) * scale


def stage_a_tables(l, n1, sub, scale):
    n2 = l // n1
    j = np.arange(n2 // sub)[:, None, None, None]
    k1 = np.arange(n1)[None, :, None, None]
    a = np.arange(sub)[None, None, :, None]
    m1 = np.arange(n1)[None, None, None, :]
    ang = 2.0 * np.pi * ((k1 * (m1 * n2 + j * sub + a)) % l) / l
    eye = np.eye(sub)
    c = np.einsum("jkam,ab->jkamb", np.cos(ang) * scale, eye).reshape(n2 // sub, n1 * sub, n1 * sub)
    s = np.einsum("jkam,ab->jkamb", np.sin(ang) * scale, eye).reshape(n2 // sub, n1 * sub, n1 * sub)
    return np.concatenate([c, -s], axis=1), np.concatenate([-s, -c], axis=1)


def _fourier_a_kernel(cc_ref, sc_ref, ka_ref, kb_ref, u_ref, yr_ref, yi_ref, *, gd):
    rows = ka_ref.shape[2]
    u = u_ref[0].reshape(rows, u_ref.shape[3])
    groups = range(u.shape[1] // gd)
    uc = jnp.concatenate([_dot(u[:, g * gd:(g + 1) * gd], cc_ref[...]) for g in groups], axis=1)
    us = jnp.concatenate([_dot(u[:, g * gd:(g + 1) * gd], sc_ref[...]) for g in groups], axis=1)
    y = _dot(ka_ref[0], uc.astype(BF16)) + _dot(kb_ref[0], us.astype(BF16))
    yr_ref[0] = y[:rows].astype(yr_ref.dtype).reshape(yr_ref.shape[1:])
    yi_ref[0] = y[rows:].astype(yi_ref.dtype).reshape(yi_ref.shape[1:])


def fourier_a(p4, cos_c, sin_c, ka, kb, *, col_block, width):
    b, n1, n2, _ = p4.shape
    gd = cos_c.shape[0]
    tiles, rows2, rows = ka.shape
    sub = rows // n1
    tab = pl.BlockSpec((1, rows2, rows), lambda bi, j: (j, 0, 0))
    out_spec = pl.BlockSpec((1, n1, sub, width), lambda bi, j: (bi, 0, j, 0))
    out = jax.ShapeDtypeStruct((b, n1, n2, width), BF16)
    return pl.pallas_call(
        functools.partial(_fourier_a_kernel, gd=gd),
        grid=(b, tiles),
        in_specs=[_resident((gd, gd)), _resident((gd, gd)), tab, tab,
                  pl.BlockSpec((1, n1, sub, width), lambda bi, j: (bi, 0, j, col_block))],
        out_specs=[out_spec, out_spec],
        out_shape=[out, out],
        compiler_params=_params(("arbitrary", "arbitrary")),
        name="fourier_a",
    )(cos_c, sin_c, ka, kb, p4)


def _fourier_c_kernel(c_ref, s_ref, yr_ref, yi_ref, o_ref):
    o = _dot(c_ref[...], yr_ref[0, 0]) + _dot(s_ref[...], yi_ref[0, 0])
    o_ref[0, 0] = o.astype(o_ref.dtype)


def fourier_c(yr4, yi4, cos2, sin2):
    b, n1, n2, width = yr4.shape
    data = pl.BlockSpec((1, 1, n2, width), lambda bi, k: (bi, k, 0, 0))
    return pl.pallas_call(
        _fourier_c_kernel,
        grid=(b, n1),
        in_specs=[_resident((n2, n2)), _resident((n2, n2)), data, data],
        out_specs=data,
        out_shape=jax.ShapeDtypeStruct((b, n1, n2, width), BF16),
        compiler_params=_params(("arbitrary", "arbitrary")),
        name="fourier_c",
    )(cos2, sin2, yr4, yi4)


def position_order_matrix(n1, sub):
    p = np.zeros((sub * n1, n1 * sub))
    for k1 in range(n1):
        for k2 in range(sub):
            p[k2 * n1 + k1, k1 * sub + k2] = 1.0
    return p


def _merge_kernel(a_ref, f_ref, ga_ref, gf_ref, x_ref, gt_ref, pm_ref, wa_ref, wf_ref, wo_ref, o_ref):
    ya = _dot(a_ref[...], wa_ref[...])
    f = f_ref[0].reshape(pm_ref.shape[1], f_ref.shape[3])
    yf = _dot(_dot(pm_ref[...], f).astype(BF16), wf_ref[...])
    merged = ga_ref[...].astype(F32) * ya + gf_ref[...].astype(F32) * yf
    y = _dot(merged.astype(BF16), wo_ref[...])
    o_ref[...] = x_ref[...] + gt_ref[0] * y


def merge(attn2d, four4, pm, p2d, x2d, mod3, wa, wf, wo, *, rows_per_mod, gate_col0):
    m, d = x2d.shape
    dw = attn2d.shape[1]
    _, n1, _, fw = four4.shape
    tm = pm.shape[0]
    sub = tm // n1
    tiles_per_mod = rows_per_mod // tm
    ga_blk = gate_col0 // d
    return pl.pallas_call(
        _merge_kernel,
        grid=(m // tm,),
        in_specs=[pl.BlockSpec((tm, dw), lambda i: (i, 0)),
                  pl.BlockSpec((1, n1, sub, fw), lambda i: (i // tiles_per_mod, 0, i % tiles_per_mod, 0)),
                  pl.BlockSpec((tm, d), lambda i: (i, ga_blk)),
                  pl.BlockSpec((tm, d), lambda i: (i, ga_blk + 1)),
                  pl.BlockSpec((tm, d), lambda i: (i, 0)),
                  pl.BlockSpec((1, 1, d), lambda i: ((i // tiles_per_mod) * N_MOD + 2, 0, 0)),
                  _resident(pm.shape), _resident(wa.shape), _resident(wf.shape), _resident(wo.shape)],
        out_specs=pl.BlockSpec((tm, d), lambda i: (i, 0)),
        out_shape=jax.ShapeDtypeStruct((m, d), F32),
        compiler_params=_params(("arbitrary",)),
        name="merge",
    )(attn2d, four4, p2d, p2d, x2d, mod3, pm, wa, wf, wo)


def _mlp_kernel(x_ref, g_ref, sh_ref, sc_ref, gt_ref, gf_ref, w1_ref, w2_ref, o_ref, h_ref, acc_ref):
    j = pl.program_id(1)
    last = pl.num_programs(1) - 1
    n_sub = x_ref.shape[0] // ROW_SUB

    def run(first, final):
        zs = {}

        def issue(idx):
            rows = slice(idx * ROW_SUB, (idx + 1) * ROW_SUB)
            if first:
                h = _norm_mod(x_ref[rows], g_ref[...], sh_ref[0], sc_ref[0]).astype(BF16)
                h_ref[rows] = h
            else:
                h = h_ref[rows]
            zs[idx] = jnp.maximum(_dot(h, w1_ref[...]), 0.0)

        issue(0)
        for idx in range(n_sub):
            if idx + 1 < n_sub:
                issue(idx + 1)
            z = zs.pop(idx)
            rows = slice(idx * ROW_SUB, (idx + 1) * ROW_SUB)
            total = _dot((z * z).astype(BF16), w2_ref[...])
            if not first:
                total = acc_ref[rows] + total
            if final:
                y = x_ref[rows] + gt_ref[0] * total
                o_ref[rows] = _rms(y) * gf_ref[...]
            else:
                acc_ref[rows] = total

    pl.when(j == 0)(functools.partial(run, True, False))
    pl.when((j > 0) & (j < last))(functools.partial(run, False, False))
    pl.when(j == last)(functools.partial(run, False, True))


def mlp(x2d, g2, mod3, g_final, w1, w2, *, rows_per_mod, tm, tf):
    m, d = x2d.shape
    f = w1.shape[1]
    assert f // tf >= 2 and tm % ROW_SUB == 0
    tiles_per_mod = rows_per_mod // tm

    def mod_map(which):
        return lambda i, j: ((i // tiles_per_mod) * N_MOD + which, 0, 0)

    return pl.pallas_call(
        _mlp_kernel,
        grid=(m // tm, f // tf),
        in_specs=[pl.BlockSpec((tm, d), lambda i, j: (i, 0)),
                  pl.BlockSpec((1, d), lambda i, j: (0, 0)),
                  pl.BlockSpec((1, 1, d), mod_map(3)),
                  pl.BlockSpec((1, 1, d), mod_map(4)),
                  pl.BlockSpec((1, 1, d), mod_map(5)),
                  pl.BlockSpec((1, d), lambda i, j: (0, 0)),
                  pl.BlockSpec((d, tf), lambda i, j: (0, j)),
                  pl.BlockSpec((tf, d), lambda i, j: (j, 0))],
        out_specs=pl.BlockSpec((tm, d), lambda i, j: (i, 0), pipeline_mode=pl.Buffered(1)),
        out_shape=jax.ShapeDtypeStruct((m, d), F32),
        scratch_shapes=[pltpu.VMEM((tm, d), BF16), pltpu.VMEM((tm, d), F32)],
        compiler_params=_params(("arbitrary", "arbitrary")),
        name="mlp",
    )(x2d, g2.reshape(1, d), mod3, mod3, mod3, g_final.reshape(1, d), w1, w2)


def kernel(x, c, ctx, c_ctx, w_ada, b_ada, g_norm1, w_in, lam_q1, lam_k1, lam_q2, lam_k2, g_subln,
           w_attn_br, w_four_br, w_out, g_norm2, w_mlp_in, w_mlp_out, g_final):
    b, l, d = x.shape
    n_ctx = ctx.shape[1]
    assert w_ada.shape[0] == 1, "single-layer block"
    dw = w_attn_br.shape[1]
    fw = w_four_br.shape[1]
    n_heads = dw // HEAD_COLS
    gd = fw // N_FOURIER_GROUPS
    in_cols = w_in.shape[2]
    gate_col0 = 3 * dw + fw
    assert in_cols == gate_col0 + 2 * d

    cvec = jnp.zeros((8, d), F32).at[:b].set(c).at[b].set(c_ctx)
    mod = ada_mod(cvec, w_ada[0], b_ada[0])
    mod3 = mod[:b + 1].reshape((b + 1) * N_MOD, 1, d)

    tabs = rope_tables(l)
    tm, tn = 1024, 1024
    x2d = x.reshape(b * l, d)
    p2d = in_proj(x2d, g_norm1[0], mod3, 0, w_in[0], tabs, rows_per_mod=l, col_tile0=0,
                  n_cols=in_cols, rope_tiles=2 * dw // tn, gate_tile0=gate_col0 // tn, tm=tm, tn=tn)
    pc2d = in_proj(ctx.reshape(b * n_ctx, d), g_norm1[0], mod3, b, w_in[0], tabs,
                   rows_per_mod=b * n_ctx, col_tile0=dw // tn, n_cols=2 * dw,
                   rope_tiles=0, gate_tile0=0, tm=b * n_ctx, tn=tn)

    lam_vecs = [v[0].astype(F32).reshape(1, HEAD_DIM) for v in (lam_q1, lam_k1, lam_q2, lam_k2)]
    attn, (wa_bf, wf_bf, wo_bf, w1_bf, w2_bf) = diff_attn(
        p2d.reshape(b, l, in_cols), pc2d.reshape(b, n_ctx, 2 * dw), lam_vecs, g_subln[0],
        [w_attn_br[0], w_four_br[0], w_out[0], w_mlp_in[0], w_mlp_out[0]], n_heads=n_heads, tq=512)

    n1 = FOURIER_N1
    n2 = l // n1
    sub = FOURIER_N1
    assert n1 * n2 == l and n2 % sub == 0
    cos_c, sin_c = (_bf16_const(t) for t in dft_tables(gd, gd ** -0.5))
    ka, kb = (_bf16_const(t) for t in stage_a_tables(l, n1, sub, l ** -0.5))
    cos2, sin2 = (_bf16_const(t) for t in dft_tables(n2, 1.0))
    yr, yi = fourier_a(p2d.reshape(b, n1, n2, in_cols), cos_c, sin_c, ka, kb,
                       col_block=3 * dw // fw, width=fw)
    four4 = fourier_c(yr, yi, cos2, sin2)

    x_mid = merge(attn.reshape(b * l, dw), four4, _bf16_const(position_order_matrix(n1, sub)), p2d, x2d,
                  mod3, wa_bf, wf_bf, wo_bf, rows_per_mod=l, gate_col0=gate_col0)
    out = mlp(x_mid, g_norm2[0], mod3, g_final, w1_bf, w2_bf,
              rows_per_mod=l, tm=1024, tf=512)
    return out.reshape(b, l, d)
```

```python
import functools
import math

import jax
import jax.numpy as jnp
import numpy as np
from jax import lax
from jax.experimental import pallas as pl
from jax.experimental.pallas import tpu as pltpu

GRID_W = 64
HEAD_DIM = 64
HEAD_COLS = 2 * HEAD_DIM
N_FOURIER_GROUPS = 4
N_MOD = 6
ROPE_BASE = 10000.0
EPS = 1e-6
LAM_INIT = 0.8 - 0.6 * math.exp(-0.3 * 0)

V7X_VMEM_LIMIT_BYTES = 58 * 1024 * 1024
BF16 = jnp.bfloat16
BF16_ROW_TILE = 16
F32 = jnp.float32


def _params(semantics, flags=None):
    return pltpu.CompilerParams(dimension_semantics=semantics,
                                vmem_limit_bytes=V7X_VMEM_LIMIT_BYTES, flags=flags)


def _dot(a, b):
    return jnp.dot(a, b, preferred_element_type=F32)


def _dot_nt(a, b):
    return lax.dot_general(a, b, (((1,), (1,)), ((), ())), preferred_element_type=F32)


def _resident(shape):
    zeros = (0,) * len(shape)
    return pl.BlockSpec(shape, lambda *_: zeros, pipeline_mode=pl.Buffered(1))


def _ada_kernel(c_ref, w_ref, b_ref, o_ref):
    cv = c_ref[...]
    o_ref[...] = _dot(cv * jax.nn.sigmoid(cv), w_ref[...]) + b_ref[...]


def ada_mod(cvec, w_ada, b_ada, *, tn=1024):
    rows, d = cvec.shape
    n = w_ada.shape[1]
    return pl.pallas_call(
        _ada_kernel,
        grid=(n // tn,),
        in_specs=[pl.BlockSpec((rows, d), lambda j: (0, 0)),
                  pl.BlockSpec((d, tn), lambda j: (0, j)),
                  pl.BlockSpec((1, tn), lambda j: (0, j))],
        out_specs=pl.BlockSpec((rows, tn), lambda j: (0, j)),
        out_shape=jax.ShapeDtypeStruct((rows, n), F32),
        compiler_params=_params(("arbitrary",)),
        name="ada_mod",
    )(cvec, w_ada, b_ada.reshape(1, n))


def _rms(x):
    return x * lax.rsqrt(jnp.mean(x * x, axis=-1, keepdims=True) + EPS)


def _norm_mod(x, g, shift, scale):
    return _rms(x) * g * (1.0 + scale) + shift


ROW_SUB = 256


def _rope(xs, cos, sin_lo, sin_hi):
    half = HEAD_DIM // 4
    up = pltpu.roll(xs, HEAD_COLS - half, 1)
    dn = pltpu.roll(xs, half, 1)
    return xs * cos + up * sin_lo + dn * sin_hi


def _in_proj_kernel(x_ref, g_ref, sh_ref, sc_ref, w_ref, tab_ref, o_ref, h_ref, *,
                    rope_tiles, gate_tile0):
    j = pl.program_id(1)
    tm, tn = o_ref.shape
    n_sub = tm // ROW_SUB

    def run(first, kind):
        w = w_ref[...].astype(BF16)
        accs = {}

        def issue(idx):
            rows = slice(idx * ROW_SUB, (idx + 1) * ROW_SUB)
            if first:
                h = _norm_mod(x_ref[rows], g_ref[...], sh_ref[0], sc_ref[0]).astype(BF16)
                h_ref[rows] = h
            else:
                h = h_ref[rows]
            accs[idx] = _dot(h, w)

        issue(0)
        for idx in range(n_sub):
            if idx + 1 < n_sub:
                issue(idx + 1)
            acc = accs.pop(idx)
            rows = slice(idx * ROW_SUB, (idx + 1) * ROW_SUB)
            if kind == "rope":
                tab = tab_ref[0, rows]
                cos = tab[:, :HEAD_COLS]
                sin_lo = tab[:, HEAD_COLS:2 * HEAD_COLS]
                sin_hi = tab[:, 2 * HEAD_COLS:]
                for c in range(tn // HEAD_COLS):
                    cols = slice(c * HEAD_COLS, (c + 1) * HEAD_COLS)
                    o_ref[rows, cols] = _rope(acc[:, cols], cos, sin_lo, sin_hi).astype(o_ref.dtype)
            elif kind == "gate":
                o_ref[rows] = jax.nn.sigmoid(acc).astype(o_ref.dtype)
            else:
                o_ref[rows] = acc.astype(o_ref.dtype)

    if rope_tiles:
        pl.when(j == 0)(functools.partial(run, True, "rope"))
        pl.when((j > 0) & (j < rope_tiles))(functools.partial(run, False, "rope"))
        pl.when((j >= rope_tiles) & (j < gate_tile0))(functools.partial(run, False, "plain"))
        pl.when(j >= gate_tile0)(functools.partial(run, False, "gate"))
    else:
        pl.when(j == 0)(functools.partial(run, True, "plain"))
        pl.when(j > 0)(functools.partial(run, False, "plain"))


def in_proj(x2d, g, mod3, mod_row0, w, tabs, *, rows_per_mod, col_tile0, n_cols,
            rope_tiles, gate_tile0, tm, tn):
    m, d = x2d.shape
    tiles_per_mod = rows_per_mod // tm
    q_tiles = rope_tiles // 2 if rope_tiles else 1
    ltiles = tabs.shape[1] // tm if rope_tiles else 1

    def mod_map(which):
        return lambda i, j: ((mod_row0 + i // tiles_per_mod) * N_MOD + which, 0, 0)

    if rope_tiles:
        tab_spec = pl.BlockSpec((1, tm, 3 * HEAD_COLS),
                                lambda i, j: (jnp.minimum(j // q_tiles, 1), i % ltiles, 0))
    else:
        tab_spec = pl.BlockSpec((1, 8, 3 * HEAD_COLS), lambda i, j: (0, 0, 0))

    kern = functools.partial(_in_proj_kernel, rope_tiles=rope_tiles, gate_tile0=gate_tile0)
    return pl.pallas_call(
        kern,
        grid=(m // tm, n_cols // tn),
        in_specs=[pl.BlockSpec((tm, d), lambda i, j: (i, 0)),
                  pl.BlockSpec((1, d), lambda i, j: (0, 0)),
                  pl.BlockSpec((1, 1, d), mod_map(0)),
                  pl.BlockSpec((1, 1, d), mod_map(1)),
                  pl.BlockSpec((d, tn), lambda i, j: (0, j + col_tile0)),
                  tab_spec],
        out_specs=pl.BlockSpec((tm, tn), lambda i, j: (i, j)),
        out_shape=jax.ShapeDtypeStruct((m, n_cols), BF16),
        scratch_shapes=[pltpu.VMEM((tm, d), BF16)],
        compiler_params=_params(("arbitrary", "arbitrary")),
        name="in_proj",
    )(x2d, g.reshape(1, d), mod3, mod3, w, tabs)


def rope_tables(seq_len):
    pos = np.arange(seq_len)
    row = pos // GRID_W
    col = pos % GRID_W
    half = HEAD_DIM // 2
    inv_freq = ROPE_BASE ** (-np.arange(0, half, 2, dtype=np.float64) / half)

    def cs(p):
        ang = p[:, None].astype(np.float64) * inv_freq[None, :]
        ang = np.concatenate([ang, ang], axis=-1)
        return np.cos(ang), np.sin(ang)

    cr, sr = cs(row)
    cc, sc = cs(col)
    cos64 = np.concatenate([cr, cc], axis=-1)
    sin64 = np.concatenate([sr, sc], axis=-1)
    lower = (np.arange(HEAD_DIM) % half) < (half // 2)
    sin_lo64 = np.where(lower[None, :], -sin64, 0.0)
    sin_hi64 = np.where(lower[None, :], 0.0, sin64)
    tab = np.concatenate([np.tile(cos64, (1, 2)), np.tile(sin_lo64, (1, 2)),
                          np.tile(sin_hi64, (1, 2))], axis=-1)
    q_scale = HEAD_DIM ** -0.5 * math.log2(math.e)
    return jnp.asarray(np.stack([tab * q_scale, tab]), dtype=F32)


ATTN_KEY_CHUNK = 512
ATTN_LOOKAHEAD = 2


def _attn_kernel(lq1_ref, lk1_ref, lq2_ref, lk2_ref, gs_ref, q_ref, k_ref, kc_ref, v_ref, vc_ref,
                 *rest, n_cast):
    w_refs, o_ref, wbf_refs, vt_ref = rest[:n_cast], rest[n_cast], rest[n_cast + 1:-1], rest[-1]
    n_lat = k_ref.shape[1]

    for w_ref, wbf_ref in zip(w_refs, wbf_refs):
        wbf_ref[...] = w_ref[...].astype(wbf_ref.dtype)

    @pl.when(pl.program_id(2) == 0)
    def _():
        for off in range(0, n_lat, ATTN_KEY_CHUNK):
            vt_ref[:, off:off + ATTN_KEY_CHUNK] = (
                v_ref[0, off:off + ATTN_KEY_CHUNK].astype(F32).T.astype(BF16))
        vt_ref[:, n_lat:] = vc_ref[0].astype(F32).T.astype(BF16)

    lam = (jnp.exp(jnp.sum(lq1_ref[...] * lk1_ref[...], axis=-1, keepdims=True))
           - jnp.exp(jnp.sum(lq2_ref[...] * lk2_ref[...], axis=-1, keepdims=True))
           + LAM_INIT)
    q = q_ref[0]
    lane = lax.broadcasted_iota(jnp.int32, q.shape, 1)
    qms = [jnp.where((lane < HEAD_DIM) == (comp == 0), q, jnp.zeros_like(q)) for comp in range(2)]
    chunks = ([(off, ATTN_KEY_CHUNK) for off in range(0, n_lat, ATTN_KEY_CHUNK)]
              + [(n_lat, kc_ref.shape[1])])
    items = [(off, width, comp) for off, width in chunks for comp in range(2)]
    ahead = 2 * ATTN_LOOKAHEAD
    scores = {}

    def issue(i):
        off, width, comp = items[i]
        keys = k_ref[0, off:off + width] if off < n_lat else kc_ref[0]
        scores[i] = _dot_nt(keys, qms[comp])

    for i in range(min(ahead, len(items))):
        issue(i)
    m = [None, None]
    d = [None, None]
    acc = [None, None]
    for i, (off, width, comp) in enumerate(items):
        if i + ahead < len(items):
            issue(i + ahead)
        s = scores.pop(i)
        vals_t = vt_ref[:, off:off + width]
        col_max = jnp.max(s, axis=0, keepdims=True)
        if m[comp] is None:
            m[comp] = col_max
            p = jnp.exp2(s - col_max)
            d[comp] = jnp.sum(p, axis=0, keepdims=True)
            acc[comp] = _dot(vals_t, p.astype(BF16))
        else:
            m_new = jnp.maximum(m[comp], col_max)
            alpha = jnp.exp2(m[comp] - m_new)
            p = jnp.exp2(s - m_new)
            d[comp] = alpha * d[comp] + jnp.sum(p, axis=0, keepdims=True)
            acc[comp] = alpha * acc[comp] + _dot(vals_t, p.astype(BF16))
            m[comp] = m_new
    o_t = acc[0] * (1.0 / d[0]) - acc[1] * (lam / d[1])
    o = _rms(o_t.T) * gs_ref[...] * (1.0 - LAM_INIT)
    o_ref[0] = o.astype(o_ref.dtype)


def diff_attn(p3, pc3, lam_vecs, g_subln, cast_weights, *, n_heads, tq):
    b, l, _ = p3.shape
    ctx = pc3.shape[1]
    h = n_heads
    nq = l // tq
    n_steps = b * h * nq
    vec = pl.BlockSpec((1, HEAD_DIM), lambda bi, hi, qi: (0, 0))

    def slab_spec(w):
        rows, cols = w.shape
        blk = max(BF16_ROW_TILE, rows // n_steps)
        last = rows // blk - 1
        return pl.BlockSpec(
            (blk, cols), lambda bi, hi, qi: (jnp.minimum((bi * h + hi) * nq + qi, last), 0))

    slabs = [slab_spec(w) for w in cast_weights]
    outs = pl.pallas_call(
        functools.partial(_attn_kernel, n_cast=len(cast_weights)),
        grid=(b, h, nq),
        in_specs=[vec, vec, vec, vec,
                  pl.BlockSpec((1, HEAD_COLS), lambda bi, hi, qi: (0, 0)),
                  pl.BlockSpec((1, tq, HEAD_COLS), lambda bi, hi, qi: (bi, qi, hi)),
                  pl.BlockSpec((1, l, HEAD_COLS), lambda bi, hi, qi: (bi, 0, h + hi)),
                  pl.BlockSpec((1, ctx, HEAD_COLS), lambda bi, hi, qi: (bi, 0, hi)),
                  pl.BlockSpec((1, l, HEAD_COLS), lambda bi, hi, qi: (bi, 0, 2 * h + hi)),
                  pl.BlockSpec((1, ctx, HEAD_COLS), lambda bi, hi, qi: (bi, 0, h + hi))] + slabs,
        out_specs=[pl.BlockSpec((1, tq, HEAD_COLS), lambda bi, hi, qi: (bi, qi, hi))] + slabs,
        out_shape=[jax.ShapeDtypeStruct((b, l, h * HEAD_COLS), BF16)]
                  + [jax.ShapeDtypeStruct(w.shape, BF16) for w in cast_weights],
        scratch_shapes=[pltpu.VMEM((HEAD_COLS, l + ctx), BF16)],
        compiler_params=_params(("arbitrary", "arbitrary", "arbitrary")),
        name="diff_attn",
    )(*lam_vecs, g_subln.reshape(1, HEAD_COLS), p3, p3, pc3, p3, pc3, *cast_weights)
    return outs[0], outs[1:]


FOURIER_N1 = 16


def _bf16_const(a):
    return jnp.asarray(a, dtype=F32).astype(BF16)


def dft_tables(n, scale):
    idx = (np.arange(n)[:, None] * np.arange(n)[None, :]) % n
    ang = 2.0 * np.pi * idx / n
    return np.cos(ang) * scale, np.sin(ang) * scale


def stage_a_tables(l, n1, sub, scale):
    n2 = l // n1
    j = np.arange(n2 // sub)[:, None, None, None]
    k1 = np.arange(n1)[None, :, None, None]
    a = np.arange(sub)[None, None, :, None]
    m1 = np.arange(n1)[None, None, None, :]
    ang = 2.0 * np.pi * ((k1 * (m1 * n2 + j * sub + a)) % l) / l
    eye = np.eye(sub)
    c = np.einsum("jkam,ab->jkamb", np.cos(ang) * scale, eye).reshape(n2 // sub, n1 * sub, n1 * sub)
    s = np.einsum("jkam,ab->jkamb", np.sin(ang) * scale, eye).reshape(n2 // sub, n1 * sub, n1 * sub)
    return np.concatenate([c, -s], axis=1), np.concatenate([-s, -c], axis=1)


def _fourier_a_kernel(cc_ref, sc_ref, ka_ref, kb_ref, u_ref, yr_ref, yi_ref, *, gd, sub):
    rows = ka_ref.shape[2]
    for t in range(ka_ref.shape[0]):
        inner = slice(t * sub, (t + 1) * sub)
        u = u_ref[0, :, inner].reshape(rows, u_ref.shape[3])
        groups = range(u.shape[1] // gd)
        uc = jnp.concatenate([_dot(u[:, g * gd:(g + 1) * gd], cc_ref[...]) for g in groups], axis=1)
        us = jnp.concatenate([_dot(u[:, g * gd:(g + 1) * gd], sc_ref[...]) for g in groups], axis=1)
        y = _dot(ka_ref[t], uc.astype(BF16)) + _dot(kb_ref[t], us.astype(BF16))
        block = (yr_ref.shape[1], sub, yr_ref.shape[3])
        yr_ref[0, :, inner] = y[:rows].astype(yr_ref.dtype).reshape(block)
        yi_ref[0, :, inner] = y[rows:].astype(yi_ref.dtype).reshape(block)


def fourier_a(p4, cos_c, sin_c, ka, kb, *, col_block, width, tiles_per_step):
    b, n1, n2, _ = p4.shape
    gd = cos_c.shape[0]
    tiles, rows2, rows = ka.shape
    sub = rows // n1
    tps = tiles_per_step
    tab = pl.BlockSpec((tps, rows2, rows), lambda bi, j: (j, 0, 0))
    out_spec = pl.BlockSpec((1, n1, tps * sub, width), lambda bi, j: (bi, 0, j, 0))
    out = jax.ShapeDtypeStruct((b, n1, n2, width), BF16)
    return pl.pallas_call(
        functools.partial(_fourier_a_kernel, gd=gd, sub=sub),
        grid=(b, tiles // tps),
        in_specs=[_resident((gd, gd)), _resident((gd, gd)), tab, tab,
                  pl.BlockSpec((1, n1, tps * sub, width), lambda bi, j: (bi, 0, j, col_block))],
        out_specs=[out_spec, out_spec],
        out_shape=[out, out],
        compiler_params=_params(("arbitrary", "arbitrary")),
        name="fourier_a",
    )(cos_c, sin_c, ka, kb, p4)


def _fourier_c_kernel(c_ref, s_ref, yr_ref, yi_ref, o_ref):
    for t in range(yr_ref.shape[1]):
        o = _dot(c_ref[...], yr_ref[0, t]) + _dot(s_ref[...], yi_ref[0, t])
        o_ref[0, t] = o.astype(o_ref.dtype)


def fourier_c(yr4, yi4, cos2, sin2, *, k1_per_step):
    b, n1, n2, width = yr4.shape
    data = pl.BlockSpec((1, k1_per_step, n2, width), lambda bi, k: (bi, k, 0, 0))
    return pl.pallas_call(
        _fourier_c_kernel,
        grid=(b, n1 // k1_per_step),
        in_specs=[_resident((n2, n2)), _resident((n2, n2)), data, data],
        out_specs=data,
        out_shape=jax.ShapeDtypeStruct((b, n1, n2, width), BF16),
        compiler_params=_params(("arbitrary", "arbitrary")),
        name="fourier_c",
    )(cos2, sin2, yr4, yi4)


def position_order_matrix(n1, sub):
    p = np.zeros((sub * n1, n1 * sub))
    for k1 in range(n1):
        for k2 in range(sub):
            p[k2 * n1 + k1, k1 * sub + k2] = 1.0
    return p


def _merge_kernel(a_ref, f_ref, ga_ref, gf_ref, x_ref, gt_ref, pm_ref, wa_ref, wf_ref, wo_ref, o_ref):
    ya = _dot(a_ref[...], wa_ref[...])
    f = f_ref[0].reshape(pm_ref.shape[1], f_ref.shape[3])
    yf = _dot(_dot(pm_ref[...], f).astype(BF16), wf_ref[...])
    merged = ga_ref[...].astype(F32) * ya + gf_ref[...].astype(F32) * yf
    y = _dot(merged.astype(BF16), wo_ref[...])
    o_ref[...] = x_ref[...] + gt_ref[0] * y


def merge(attn2d, four4, pm, p2d, x2d, mod3, wa, wf, wo, *, rows_per_mod, gate_col0):
    m, d = x2d.shape
    dw = attn2d.shape[1]
    _, n1, _, fw = four4.shape
    tm = pm.shape[0]
    sub = tm // n1
    tiles_per_mod = rows_per_mod // tm
    ga_blk = gate_col0 // d
    return pl.pallas_call(
        _merge_kernel,
        grid=(m // tm,),
        in_specs=[pl.BlockSpec((tm, dw), lambda i: (i, 0)),
                  pl.BlockSpec((1, n1, sub, fw), lambda i: (i // tiles_per_mod, 0, i % tiles_per_mod, 0)),
                  pl.BlockSpec((tm, d), lambda i: (i, ga_blk)),
                  pl.BlockSpec((tm, d), lambda i: (i, ga_blk + 1)),
                  pl.BlockSpec((tm, d), lambda i: (i, 0)),
                  pl.BlockSpec((1, 1, d), lambda i: ((i // tiles_per_mod) * N_MOD + 2, 0, 0)),
                  _resident(pm.shape), _resident(wa.shape), _resident(wf.shape), _resident(wo.shape)],
        out_specs=pl.BlockSpec((tm, d), lambda i: (i, 0)),
        out_shape=jax.ShapeDtypeStruct((m, d), F32),
        compiler_params=_params(("arbitrary",)),
        name="merge",
    )(attn2d, four4, p2d, p2d, x2d, mod3, pm, wa, wf, wo)


def _mlp_kernel(x_ref, g_ref, sh_ref, sc_ref, gt_ref, gf_ref, w1_ref, w2_ref, o_ref, h_ref, acc_ref):
    j = pl.program_id(1)
    last = pl.num_programs(1) - 1
    n_sub = x_ref.shape[0] // ROW_SUB

    def run(first, final):
        zs = {}

        def issue(idx):
            rows = slice(idx * ROW_SUB, (idx + 1) * ROW_SUB)
            if first:
                h = _norm_mod(x_ref[rows], g_ref[...], sh_ref[0], sc_ref[0]).astype(BF16)
                h_ref[rows] = h
            else:
                h = h_ref[rows]
            zs[idx] = jnp.maximum(_dot(h, w1_ref[...]), 0.0)

        issue(0)
        for idx in range(n_sub):
            if idx + 1 < n_sub:
                issue(idx + 1)
            z = zs.pop(idx)
            rows = slice(idx * ROW_SUB, (idx + 1) * ROW_SUB)
            total = _dot((z * z).astype(BF16), w2_ref[...])
            if not first:
                total = acc_ref[rows] + total
            if final:
                y = x_ref[rows] + gt_ref[0] * total
                o_ref[rows] = _rms(y) * gf_ref[...]
            else:
                acc_ref[rows] = total

    pl.when(j == 0)(functools.partial(run, True, False))
    pl.when((j > 0) & (j < last))(functools.partial(run, False, False))
    pl.when(j == last)(functools.partial(run, False, True))


def mlp(x2d, g2, mod3, g_final, w1, w2, *, rows_per_mod, tm, tf):
    m, d = x2d.shape
    f = w1.shape[1]
    assert f // tf >= 2 and tm % ROW_SUB == 0
    tiles_per_mod = rows_per_mod // tm

    def mod_map(which):
        return lambda i, j: ((i // tiles_per_mod) * N_MOD + which, 0, 0)

    return pl.pallas_call(
        _mlp_kernel,
        grid=(m // tm, f // tf),
        in_specs=[pl.BlockSpec((tm, d), lambda i, j: (i, 0)),
                  pl.BlockSpec((1, d), lambda i, j: (0, 0)),
                  pl.BlockSpec((1, 1, d), mod_map(3)),
                  pl.BlockSpec((1, 1, d), mod_map(4)),
                  pl.BlockSpec((1, 1, d), mod_map(5)),
                  pl.BlockSpec((1, d), lambda i, j: (0, 0)),
                  pl.BlockSpec((d, tf), lambda i, j: (0, j)),
                  pl.BlockSpec((tf, d), lambda i, j: (j, 0))],
        out_specs=pl.BlockSpec((tm, d), lambda i, j: (i, 0), pipeline_mode=pl.Buffered(1)),
        out_shape=jax.ShapeDtypeStruct((m, d), F32),
        scratch_shapes=[pltpu.VMEM((tm, d), BF16), pltpu.VMEM((tm, d), F32)],
        compiler_params=_params(("arbitrary", "arbitrary")),
        name="mlp",
    )(x2d, g2.reshape(1, d), mod3, mod3, mod3, g_final.reshape(1, d), w1, w2)


def kernel(x, c, ctx, c_ctx, w_ada, b_ada, g_norm1, w_in, lam_q1, lam_k1, lam_q2, lam_k2, g_subln,
           w_attn_br, w_four_br, w_out, g_norm2, w_mlp_in, w_mlp_out, g_final):
    b, l, d = x.shape
    n_ctx = ctx.shape[1]
    assert w_ada.shape[0] == 1, "single-layer block"
    dw = w_attn_br.shape[1]
    fw = w_four_br.shape[1]
    n_heads = dw // HEAD_COLS
    gd = fw // N_FOURIER_GROUPS
    in_cols = w_in.shape[2]
    gate_col0 = 3 * dw + fw
    assert in_cols == gate_col0 + 2 * d

    cvec = jnp.zeros((8, d), F32).at[:b].set(c).at[b].set(c_ctx)
    mod = ada_mod(cvec, w_ada[0], b_ada[0])
    mod3 = mod[:b + 1].reshape((b + 1) * N_MOD, 1, d)

    tabs = rope_tables(l)
    tm, tn = 1024, 1024
    x2d = x.reshape(b * l, d)
    p2d = in_proj(x2d, g_norm1[0], mod3, 0, w_in[0], tabs, rows_per_mod=l, col_tile0=0,
                  n_cols=in_cols, rope_tiles=2 * dw // tn, gate_tile0=gate_col0 // tn, tm=tm, tn=tn)
    pc2d = in_proj(ctx.reshape(b * n_ctx, d), g_norm1[0], mod3, b, w_in[0], tabs,
                   rows_per_mod=b * n_ctx, col_tile0=dw // tn, n_cols=2 * dw,
                   rope_tiles=0, gate_tile0=0, tm=b * n_ctx, tn=tn)

    lam_vecs = [v[0].astype(F32).reshape(1, HEAD_DIM) for v in (lam_q1, lam_k1, lam_q2, lam_k2)]
    attn, (wa_bf, wf_bf, wo_bf, w1_bf, w2_bf) = diff_attn(
        p2d.reshape(b, l, in_cols), pc2d.reshape(b, n_ctx, 2 * dw), lam_vecs, g_subln[0],
        [w_attn_br[0], w_four_br[0], w_out[0], w_mlp_in[0], w_mlp_out[0]], n_heads=n_heads, tq=512)

    n1 = FOURIER_N1
    n2 = l // n1
    sub = FOURIER_N1
    assert n1 * n2 == l and n2 % sub == 0
    cos_c, sin_c = (_bf16_const(t) for t in dft_tables(gd, gd ** -0.5))
    ka, kb = (_bf16_const(t) for t in stage_a_tables(l, n1, sub, l ** -0.5))
    cos2, sin2 = (_bf16_const(t) for t in dft_tables(n2, 1.0))
    yr, yi = fourier_a(p2d.reshape(b, n1, n2, in_cols), cos_c, sin_c, ka, kb,
                       col_block=3 * dw // fw, width=fw, tiles_per_step=4)
    four4 = fourier_c(yr, yi, cos2, sin2, k1_per_step=4)

    x_mid = merge(attn.reshape(b * l, dw), four4, _bf16_const(position_order_matrix(n1, sub)), p2d, x2d,
                  mod3, wa_bf, wf_bf, wo_bf, rows_per_mod=l, gate_col0=gate_col0)
    out = mlp(x_mid, g_norm2[0], mod3, g_final, w1_bf, w2_bf,
              rows_per_mod=l, tm=1024, tf=512)
    return out.reshape(b, l, d)
```

```python
import functools
import math

import jax
import jax.numpy as jnp
import numpy as np
from jax import lax
from jax.experimental import pallas as pl
from jax.experimental.pallas import tpu as pltpu

GRID_W = 64
HEAD_DIM = 64
HEAD_COLS = 2 * HEAD_DIM
N_FOURIER_GROUPS = 4
N_MOD = 6
ROPE_BASE = 10000.0
EPS = 1e-6
LAM_INIT = 0.8 - 0.6 * math.exp(-0.3 * 0)

V7X_VMEM_LIMIT_BYTES = 58 * 1024 * 1024
BF16 = jnp.bfloat16
BF16_ROW_TILE = 16
F32 = jnp.float32


def _params(semantics, flags=None):
    return pltpu.CompilerParams(dimension_semantics=semantics,
                                vmem_limit_bytes=V7X_VMEM_LIMIT_BYTES, flags=flags)


def _dot(a, b):
    return jnp.dot(a, b, preferred_element_type=F32)


def _dot_nt(a, b):
    return lax.dot_general(a, b, (((1,), (1,)), ((), ())), preferred_element_type=F32)


def _resident(shape):
    zeros = (0,) * len(shape)
    return pl.BlockSpec(shape, lambda *_: zeros, pipeline_mode=pl.Buffered(1))


def _ada_kernel(c_ref, w_ref, b_ref, o_ref):
    cv = c_ref[...]
    o_ref[...] = _dot(cv * jax.nn.sigmoid(cv), w_ref[...]) + b_ref[...]


def ada_mod(cvec, w_ada, b_ada, *, tn=1024):
    rows, d = cvec.shape
    n = w_ada.shape[1]
    return pl.pallas_call(
        _ada_kernel,
        grid=(n // tn,),
        in_specs=[pl.BlockSpec((rows, d), lambda j: (0, 0)),
                  pl.BlockSpec((d, tn), lambda j: (0, j)),
                  pl.BlockSpec((1, tn), lambda j: (0, j))],
        out_specs=pl.BlockSpec((rows, tn), lambda j: (0, j)),
        out_shape=jax.ShapeDtypeStruct((rows, n), F32),
        compiler_params=_params(("arbitrary",)),
        name="ada_mod",
    )(cvec, w_ada, b_ada.reshape(1, n))


def _rms(x):
    return x * lax.rsqrt(jnp.mean(x * x, axis=-1, keepdims=True) + EPS)


def _norm_mod(x, g, shift, scale):
    return _rms(x) * g * (1.0 + scale) + shift


ROW_SUB = 256


def _rope(xs, cos, sin_lo, sin_hi):
    half = HEAD_DIM // 4
    up = pltpu.roll(xs, HEAD_COLS - half, 1)
    dn = pltpu.roll(xs, half, 1)
    return xs * cos + up * sin_lo + dn * sin_hi


def _in_proj_kernel(x_ref, g_ref, sh_ref, sc_ref, w_ref, tab_ref, o_ref, h_ref, *,
                    rope_tiles, gate_tile0):
    j = pl.program_id(1)
    tm, tn = o_ref.shape
    n_sub = tm // ROW_SUB

    def run(first, kind):
        w = w_ref[...].astype(BF16)
        accs = {}

        def issue(idx):
            rows = slice(idx * ROW_SUB, (idx + 1) * ROW_SUB)
            if first:
                h = _norm_mod(x_ref[rows], g_ref[...], sh_ref[0], sc_ref[0]).astype(BF16)
                h_ref[rows] = h
            else:
                h = h_ref[rows]
            accs[idx] = _dot(h, w)

        issue(0)
        for idx in range(n_sub):
            if idx + 1 < n_sub:
                issue(idx + 1)
            acc = accs.pop(idx)
            rows = slice(idx * ROW_SUB, (idx + 1) * ROW_SUB)
            if kind == "rope":
                tab = tab_ref[0, rows]
                cos = tab[:, :HEAD_COLS]
                sin_lo = tab[:, HEAD_COLS:2 * HEAD_COLS]
                sin_hi = tab[:, 2 * HEAD_COLS:]
                for c in range(tn // HEAD_COLS):
                    cols = slice(c * HEAD_COLS, (c + 1) * HEAD_COLS)
                    o_ref[rows, cols] = _rope(acc[:, cols], cos, sin_lo, sin_hi).astype(o_ref.dtype)
            elif kind == "gate":
                o_ref[rows] = jax.nn.sigmoid(acc).astype(o_ref.dtype)
            else:
                o_ref[rows] = acc.astype(o_ref.dtype)

    if rope_tiles:
        pl.when(j == 0)(functools.partial(run, True, "rope"))
        pl.when((j > 0) & (j < rope_tiles))(functools.partial(run, False, "rope"))
        pl.when((j >= rope_tiles) & (j < gate_tile0))(functools.partial(run, False, "plain"))
        pl.when(j >= gate_tile0)(functools.partial(run, False, "gate"))
    else:
        pl.when(j == 0)(functools.partial(run, True, "plain"))
        pl.when(j > 0)(functools.partial(run, False, "plain"))


def in_proj(x2d, g, mod3, mod_row0, w, tabs, *, rows_per_mod, col_tile0, n_cols,
            rope_tiles, gate_tile0, tm, tn):
    m, d = x2d.shape
    tiles_per_mod = rows_per_mod // tm
    q_tiles = rope_tiles // 2 if rope_tiles else 1
    ltiles = tabs.shape[1] // tm if rope_tiles else 1

    def mod_map(which):
        return lambda i, j: ((mod_row0 + i // tiles_per_mod) * N_MOD + which, 0, 0)

    if rope_tiles:
        tab_spec = pl.BlockSpec((1, tm, 3 * HEAD_COLS),
                                lambda i, j: (jnp.minimum(j // q_tiles, 1), i % ltiles, 0))
    else:
        tab_spec = pl.BlockSpec((1, 8, 3 * HEAD_COLS), lambda i, j: (0, 0, 0))

    kern = functools.partial(_in_proj_kernel, rope_tiles=rope_tiles, gate_tile0=gate_tile0)
    return pl.pallas_call(
        kern,
        grid=(m // tm, n_cols // tn),
        in_specs=[pl.BlockSpec((tm, d), lambda i, j: (i, 0)),
                  pl.BlockSpec((1, d), lambda i, j: (0, 0)),
                  pl.BlockSpec((1, 1, d), mod_map(0)),
                  pl.BlockSpec((1, 1, d), mod_map(1)),
                  pl.BlockSpec((d, tn), lambda i, j: (0, j + col_tile0)),
                  tab_spec],
        out_specs=pl.BlockSpec((tm, tn), lambda i, j: (i, j)),
        out_shape=jax.ShapeDtypeStruct((m, n_cols), BF16),
        scratch_shapes=[pltpu.VMEM((tm, d), BF16)],
        compiler_params=_params(("arbitrary", "arbitrary")),
        name="in_proj",
    )(x2d, g.reshape(1, d), mod3, mod3, w, tabs)


def rope_tables(seq_len):
    pos = np.arange(seq_len)
    row = pos // GRID_W
    col = pos % GRID_W
    half = HEAD_DIM // 2
    inv_freq = ROPE_BASE ** (-np.arange(0, half, 2, dtype=np.float64) / half)

    def cs(p):
        ang = p[:, None].astype(np.float64) * inv_freq[None, :]
        ang = np.concatenate([ang, ang], axis=-1)
        return np.cos(ang), np.sin(ang)

    cr, sr = cs(row)
    cc, sc = cs(col)
    cos64 = np.concatenate([cr, cc], axis=-1)
    sin64 = np.concatenate([sr, sc], axis=-1)
    lower = (np.arange(HEAD_DIM) % half) < (half // 2)
    sin_lo64 = np.where(lower[None, :], -sin64, 0.0)
    sin_hi64 = np.where(lower[None, :], 0.0, sin64)
    tab = np.concatenate([np.tile(cos64, (1, 2)), np.tile(sin_lo64, (1, 2)),
                          np.tile(sin_hi64, (1, 2))], axis=-1)
    q_scale = HEAD_DIM ** -0.5 * math.log2(math.e)
    return jnp.asarray(np.stack([tab * q_scale, tab]), dtype=F32)


ATTN_KEY_CHUNK = 512
ATTN_LOOKAHEAD = 2


def _attn_kernel(lq1_ref, lk1_ref, lq2_ref, lk2_ref, gs_ref, q_ref, k_ref, kc_ref, v_ref, vc_ref,
                 *rest, n_cast):
    w_refs, o_ref, wbf_refs, vt_ref = rest[:n_cast], rest[n_cast], rest[n_cast + 1:-1], rest[-1]
    n_lat = k_ref.shape[1]

    for w_ref, wbf_ref in zip(w_refs, wbf_refs):
        wbf_ref[...] = w_ref[...].astype(wbf_ref.dtype)

    @pl.when(pl.program_id(2) == 0)
    def _():
        for off in range(0, n_lat, ATTN_KEY_CHUNK):
            vt_ref[:, off:off + ATTN_KEY_CHUNK] = (
                v_ref[0, off:off + ATTN_KEY_CHUNK].astype(F32).T.astype(BF16))
        vt_ref[:, n_lat:] = vc_ref[0].astype(F32).T.astype(BF16)

    lam = (jnp.exp(jnp.sum(lq1_ref[...] * lk1_ref[...], axis=-1, keepdims=True))
           - jnp.exp(jnp.sum(lq2_ref[...] * lk2_ref[...], axis=-1, keepdims=True))
           + LAM_INIT)
    q = q_ref[0]
    lane = lax.broadcasted_iota(jnp.int32, q.shape, 1)
    qms = [jnp.where((lane < HEAD_DIM) == (comp == 0), q, jnp.zeros_like(q)) for comp in range(2)]
    chunks = ([(off, ATTN_KEY_CHUNK) for off in range(0, n_lat, ATTN_KEY_CHUNK)]
              + [(n_lat, kc_ref.shape[1])])
    items = [(off, width, comp) for off, width in chunks for comp in range(2)]
    ahead = 2 * ATTN_LOOKAHEAD
    scores = {}

    def issue(i):
        off, width, comp = items[i]
        keys = k_ref[0, off:off + width] if off < n_lat else kc_ref[0]
        scores[i] = _dot_nt(keys, qms[comp])

    for i in range(min(ahead, len(items))):
        issue(i)
    m = [None, None]
    d = [None, None]
    acc = [None, None]
    for i, (off, width, comp) in enumerate(items):
        if i + ahead < len(items):
            issue(i + ahead)
        s = scores.pop(i)
        vals_t = vt_ref[:, off:off + width]
        col_max = jnp.max(s, axis=0, keepdims=True)
        if m[comp] is None:
            m[comp] = col_max
            p = jnp.exp2(s - col_max)
            d[comp] = jnp.sum(p, axis=0, keepdims=True)
            acc[comp] = _dot(vals_t, p.astype(BF16))
        else:
            m_new = jnp.maximum(m[comp], col_max)
            alpha = jnp.exp2(m[comp] - m_new)
            p = jnp.exp2(s - m_new)
            d[comp] = alpha * d[comp] + jnp.sum(p, axis=0, keepdims=True)
            acc[comp] = alpha * acc[comp] + _dot(vals_t, p.astype(BF16))
            m[comp] = m_new
    o_t = acc[0] * (1.0 / d[0]) - acc[1] * (lam / d[1])
    o = _rms(o_t.T) * gs_ref[...] * (1.0 - LAM_INIT)
    o_ref[0] = o.astype(o_ref.dtype)


def diff_attn(p3, pc3, lam_vecs, g_subln, cast_weights, *, n_heads, tq):
    b, l, _ = p3.shape
    ctx = pc3.shape[1]
    h = n_heads
    nq = l // tq
    n_steps = b * h * nq
    vec = pl.BlockSpec((1, HEAD_DIM), lambda bi, hi, qi: (0, 0))

    def slab_spec(w):
        rows, cols = w.shape
        blk = max(BF16_ROW_TILE, rows // n_steps)
        last = rows // blk - 1
        return pl.BlockSpec(
            (blk, cols), lambda bi, hi, qi: (jnp.minimum((bi * h + hi) * nq + qi, last), 0))

    slabs = [slab_spec(w) for w in cast_weights]
    outs = pl.pallas_call(
        functools.partial(_attn_kernel, n_cast=len(cast_weights)),
        grid=(b, h, nq),
        in_specs=[vec, vec, vec, vec,
                  pl.BlockSpec((1, HEAD_COLS), lambda bi, hi, qi: (0, 0)),
                  pl.BlockSpec((1, tq, HEAD_COLS), lambda bi, hi, qi: (bi, qi, hi)),
                  pl.BlockSpec((1, l, HEAD_COLS), lambda bi, hi, qi: (bi, 0, h + hi)),
                  pl.BlockSpec((1, ctx, HEAD_COLS), lambda bi, hi, qi: (bi, 0, hi)),
                  pl.BlockSpec((1, l, HEAD_COLS), lambda bi, hi, qi: (bi, 0, 2 * h + hi)),
                  pl.BlockSpec((1, ctx, HEAD_COLS), lambda bi, hi, qi: (bi, 0, h + hi))] + slabs,
        out_specs=[pl.BlockSpec((1, tq, HEAD_COLS), lambda bi, hi, qi: (bi, qi, hi))] + slabs,
        out_shape=[jax.ShapeDtypeStruct((b, l, h * HEAD_COLS), BF16)]
                  + [jax.ShapeDtypeStruct(w.shape, BF16) for w in cast_weights],
        scratch_shapes=[pltpu.VMEM((HEAD_COLS, l + ctx), BF16)],
        compiler_params=_params(("arbitrary", "arbitrary", "arbitrary")),
        name="diff_attn",
    )(*lam_vecs, g_subln.reshape(1, HEAD_COLS), p3, p3, pc3, p3, pc3, *cast_weights)
    return outs[0], outs[1:]


FOURIER_N1 = 16


def _bf16_const(a):
    return jnp.asarray(a, dtype=F32).astype(BF16)


def dft_tables(n, scale):
    idx = (np.arange(n)[:, None] * np.arange(n)[None, :]) % n
    ang = 2.0 * np.pi * idx / n
    return np.cos(ang) * scale, np.sin(ang) * scale


def stage_a_tables(l, n1, sub, scale):
    n2 = l // n1
    j = np.arange(n2 // sub)[:, None, None, None]
    k1 = np.arange(n1)[None, :, None, None]
    a = np.arange(sub)[None, None, :, None]
    m1 = np.arange(n1)[None, None, None, :]
    ang = 2.0 * np.pi * ((k1 * (m1 * n2 + j * sub + a)) % l) / l
    eye = np.eye(sub)
    c = np.einsum("jkam,ab->jkamb", np.cos(ang) * scale, eye).reshape(n2 // sub, n1 * sub, n1 * sub)
    s = np.einsum("jkam,ab->jkamb", np.sin(ang) * scale, eye).reshape(n2 // sub, n1 * sub, n1 * sub)
    return np.concatenate([c, -s], axis=1), np.concatenate([-s, -c], axis=1)


def _fourier_a_kernel(cc_ref, sc_ref, ka_ref, kb_ref, u_ref, yr_ref, yi_ref, *, gd, sub):
    rows = ka_ref.shape[2]
    for t in range(ka_ref.shape[0]):
        inner = slice(t * sub, (t + 1) * sub)
        u = u_ref[0, :, inner].reshape(rows, u_ref.shape[3])
        groups = range(u.shape[1] // gd)
        uc = jnp.concatenate([_dot(u[:, g * gd:(g + 1) * gd], cc_ref[...]) for g in groups], axis=1)
        us = jnp.concatenate([_dot(u[:, g * gd:(g + 1) * gd], sc_ref[...]) for g in groups], axis=1)
        y = _dot(ka_ref[t], uc.astype(BF16)) + _dot(kb_ref[t], us.astype(BF16))
        block = (yr_ref.shape[1], sub, yr_ref.shape[3])
        yr_ref[0, :, inner] = y[:rows].astype(yr_ref.dtype).reshape(block)
        yi_ref[0, :, inner] = y[rows:].astype(yi_ref.dtype).reshape(block)


def fourier_a(p4, cos_c, sin_c, ka, kb, *, col_block, width, tiles_per_step):
    b, n1, n2, _ = p4.shape
    gd = cos_c.shape[0]
    tiles, rows2, rows = ka.shape
    sub = rows // n1
    tps = tiles_per_step
    tab = pl.BlockSpec((tps, rows2, rows), lambda bi, j: (j, 0, 0))
    out_spec = pl.BlockSpec((1, n1, tps * sub, width), lambda bi, j: (bi, 0, j, 0))
    out = jax.ShapeDtypeStruct((b, n1, n2, width), BF16)
    return pl.pallas_call(
        functools.partial(_fourier_a_kernel, gd=gd, sub=sub),
        grid=(b, tiles // tps),
        in_specs=[_resident((gd, gd)), _resident((gd, gd)), tab, tab,
                  pl.BlockSpec((1, n1, tps * sub, width), lambda bi, j: (bi, 0, j, col_block))],
        out_specs=[out_spec, out_spec],
        out_shape=[out, out],
        compiler_params=_params(("arbitrary", "arbitrary")),
        name="fourier_a",
    )(cos_c, sin_c, ka, kb, p4)


def _fourier_c_kernel(c_ref, s_ref, yr_ref, yi_ref, o_ref):
    for t in range(yr_ref.shape[1]):
        o = _dot(c_ref[...], yr_ref[0, t]) + _dot(s_ref[...], yi_ref[0, t])
        o_ref[0, t] = o.astype(o_ref.dtype)


def fourier_c(yr4, yi4, cos2, sin2, *, k1_per_step):
    b, n1, n2, width = yr4.shape
    data = pl.BlockSpec((1, k1_per_step, n2, width), lambda bi, k: (bi, k, 0, 0))
    return pl.pallas_call(
        _fourier_c_kernel,
        grid=(b, n1 // k1_per_step),
        in_specs=[_resident((n2, n2)), _resident((n2, n2)), data, data],
        out_specs=data,
        out_shape=jax.ShapeDtypeStruct((b, n1, n2, width), BF16),
        compiler_params=_params(("arbitrary", "arbitrary")),
        name="fourier_c",
    )(cos2, sin2, yr4, yi4)


def position_order_matrix(n1, sub):
    p = np.zeros((sub * n1, n1 * sub))
    for k1 in range(n1):
        for k2 in range(sub):
            p[k2 * n1 + k1, k1 * sub + k2] = 1.0
    return p


def _merge_kernel(a_ref, f_ref, ga_ref, gf_ref, x_ref, gt_ref, pm_ref, wa_ref, wf_ref, wo_ref, o_ref):
    ya = _dot(a_ref[...], wa_ref[...])
    f = f_ref[0].reshape(pm_ref.shape[1], f_ref.shape[3])
    yf = _dot(_dot(pm_ref[...], f).astype(BF16), wf_ref[...])
    merged = ga_ref[...].astype(F32) * ya + gf_ref[...].astype(F32) * yf
    y = _dot(merged.astype(BF16), wo_ref[...])
    o_ref[...] = x_ref[...] + gt_ref[0] * y


def merge(attn2d, four4, pm, p2d, x2d, mod3, wa, wf, wo, *, rows_per_mod, gate_col0):
    m, d = x2d.shape
    dw = attn2d.shape[1]
    _, n1, _, fw = four4.shape
    tm = pm.shape[0]
    sub = tm // n1
    tiles_per_mod = rows_per_mod // tm
    ga_blk = gate_col0 // d
    return pl.pallas_call(
        _merge_kernel,
        grid=(m // tm,),
        in_specs=[pl.BlockSpec((tm, dw), lambda i: (i, 0)),
                  pl.BlockSpec((1, n1, sub, fw), lambda i: (i // tiles_per_mod, 0, i % tiles_per_mod, 0)),
                  pl.BlockSpec((tm, d), lambda i: (i, ga_blk)),
                  pl.BlockSpec((tm, d), lambda i: (i, ga_blk + 1)),
                  pl.BlockSpec((tm, d), lambda i: (i, 0)),
                  pl.BlockSpec((1, 1, d), lambda i: ((i // tiles_per_mod) * N_MOD + 2, 0, 0)),
                  _resident(pm.shape), _resident(wa.shape), _resident(wf.shape), _resident(wo.shape)],
        out_specs=pl.BlockSpec((tm, d), lambda i: (i, 0)),
        out_shape=jax.ShapeDtypeStruct((m, d), F32),
        compiler_params=_params(("arbitrary",)),
        name="merge",
    )(attn2d, four4, p2d, p2d, x2d, mod3, pm, wa, wf, wo)


def _mlp_kernel(x_ref, g_ref, sh_ref, sc_ref, gt_ref, gf_ref, w1_ref, w2_ref, o_ref, h_ref, acc_ref):
    j = pl.program_id(1)
    last = pl.num_programs(1) - 1
    n_sub = x_ref.shape[0] // ROW_SUB

    def run(first, final):
        zs = {}

        def issue(idx):
            rows = slice(idx * ROW_SUB, (idx + 1) * ROW_SUB)
            if first:
                h = _norm_mod(x_ref[rows], g_ref[...], sh_ref[0], sc_ref[0]).astype(BF16)
                h_ref[rows] = h
            else:
                h = h_ref[rows]
            zs[idx] = jnp.maximum(_dot(h, w1_ref[...]), 0.0)

        issue(0)
        for idx in range(n_sub):
            if idx + 1 < n_sub:
                issue(idx + 1)
            z = zs.pop(idx)
            rows = slice(idx * ROW_SUB, (idx + 1) * ROW_SUB)
            total = _dot((z * z).astype(BF16), w2_ref[...])
            if not first:
                total = acc_ref[rows] + total
            if final:
                y = x_ref[rows] + gt_ref[0] * total
                o_ref[rows] = _rms(y) * gf_ref[...]
            else:
                acc_ref[rows] = total

    pl.when(j == 0)(functools.partial(run, True, False))
    pl.when((j > 0) & (j < last))(functools.partial(run, False, False))
    pl.when(j == last)(functools.partial(run, False, True))


def mlp(x2d, g2, mod3, g_final, w1, w2, *, rows_per_mod, tm, tf):
    m, d = x2d.shape
    f = w1.shape[1]
    assert f // tf >= 2 and tm % ROW_SUB == 0
    tiles_per_mod = rows_per_mod // tm

    def mod_map(which):
        return lambda i, j: ((i // tiles_per_mod) * N_MOD + which, 0, 0)

    return pl.pallas_call(
        _mlp_kernel,
        grid=(m // tm, f // tf),
        in_specs=[pl.BlockSpec((tm, d), lambda i, j: (i, 0)),
                  pl.BlockSpec((1, d), lambda i, j: (0, 0)),
                  pl.BlockSpec((1, 1, d), mod_map(3)),
                  pl.BlockSpec((1, 1, d), mod_map(4)),
                  pl.BlockSpec((1, 1, d), mod_map(5)),
                  pl.BlockSpec((1, d), lambda i, j: (0, 0)),
                  pl.BlockSpec((d, tf), lambda i, j: (0, j)),
                  pl.BlockSpec((tf, d), lambda i, j: (j, 0))],
        out_specs=pl.BlockSpec((tm, d), lambda i, j: (i, 0), pipeline_mode=pl.Buffered(1)),
        out_shape=jax.ShapeDtypeStruct((m, d), F32),
        scratch_shapes=[pltpu.VMEM((tm, d), BF16), pltpu.VMEM((tm, d), F32)],
        compiler_params=_params(("arbitrary", "arbitrary")),
        name="mlp",
    )(x2d, g2.reshape(1, d), mod3, mod3, mod3, g_final.reshape(1, d), w1, w2)


def kernel(x, c, ctx, c_ctx, w_ada, b_ada, g_norm1, w_in, lam_q1, lam_k1, lam_q2, lam_k2, g_subln,
           w_attn_br, w_four_br, w_out, g_norm2, w_mlp_in, w_mlp_out, g_final):
    b, l, d = x.shape
    n_ctx = ctx.shape[1]
    assert w_ada.shape[0] == 1, "single-layer block"
    dw = w_attn_br.shape[1]
    fw = w_four_br.shape[1]
    n_heads = dw // HEAD_COLS
    gd = fw // N_FOURIER_GROUPS
    in_cols = w_in.shape[2]
    gate_col0 = 3 * dw + fw
    assert in_cols == gate_col0 + 2 * d

    cvec = jnp.zeros((8, d), F32).at[:b].set(c).at[b].set(c_ctx)
    mod = ada_mod(cvec, w_ada[0], b_ada[0])
    mod3 = mod[:b + 1].reshape((b + 1) * N_MOD, 1, d)

    tabs = rope_tables(l)
    tm, tn = 1024, 1024
    x2d = x.reshape(b * l, d)
    p2d = in_proj(x2d, g_norm1[0], mod3, 0, w_in[0], tabs, rows_per_mod=l, col_tile0=0,
                  n_cols=in_cols, rope_tiles=2 * dw // tn, gate_tile0=gate_col0 // tn, tm=tm, tn=tn)
    pc2d = in_proj(ctx.reshape(b * n_ctx, d), g_norm1[0], mod3, b, w_in[0], tabs,
                   rows_per_mod=b * n_ctx, col_tile0=dw // tn, n_cols=2 * dw,
                   rope_tiles=0, gate_tile0=0, tm=b * n_ctx, tn=tn)

    lam_vecs = [v[0].astype(F32).reshape(1, HEAD_DIM) for v in (lam_q1, lam_k1, lam_q2, lam_k2)]
    attn, (wa_bf, wf_bf, wo_bf, w1_bf, w2_bf) = diff_attn(
        p2d.reshape(b, l, in_cols), pc2d.reshape(b, n_ctx, 2 * dw), lam_vecs, g_subln[0],
        [w_attn_br[0], w_four_br[0], w_out[0], w_mlp_in[0], w_mlp_out[0]], n_heads=n_heads, tq=1024)

    n1 = FOURIER_N1
    n2 = l // n1
    sub = FOURIER_N1
    assert n1 * n2 == l and n2 % sub == 0
    cos_c, sin_c = (_bf16_const(t) for t in dft_tables(gd, gd ** -0.5))
    ka, kb = (_bf16_const(t) for t in stage_a_tables(l, n1, sub, l ** -0.5))
    cos2, sin2 = (_bf16_const(t) for t in dft_tables(n2, 1.0))
    yr, yi = fourier_a(p2d.reshape(b, n1, n2, in_cols), cos_c, sin_c, ka, kb,
                       col_block=3 * dw // fw, width=fw, tiles_per_step=4)
    four4 = fourier_c(yr, yi, cos2, sin2, k1_per_step=4)

    x_mid = merge(attn.reshape(b * l, dw), four4, _bf16_const(position_order_matrix(n1, sub)), p2d, x2d,
                  mod3, wa_bf, wf_bf, wo_bf, rows_per_mod=l, gate_col0=gate_col0)
    out = mlp(x_mid, g_norm2[0], mod3, g_final, w1_bf, w2_bf,
              rows_per_mod=l, tm=1024, tf=1024)
    return out.reshape(b, l, d)
```

```python
import functools
import math

import jax
import jax.numpy as jnp
import numpy as np
from jax import lax
from jax.experimental import pallas as pl
from jax.experimental.pallas import tpu as pltpu

GRID_W = 64
HEAD_DIM = 64
HEAD_COLS = 2 * HEAD_DIM
N_FOURIER_GROUPS = 4
N_MOD = 6
ROPE_BASE = 10000.0
EPS = 1e-6
LAM_INIT = 0.8 - 0.6 * math.exp(-0.3 * 0)

V7X_VMEM_LIMIT_BYTES = 58 * 1024 * 1024
BF16 = jnp.bfloat16
BF16_ROW_TILE = 16
F32 = jnp.float32


def _params(semantics, flags=None):
    return pltpu.CompilerParams(dimension_semantics=semantics,
                                vmem_limit_bytes=V7X_VMEM_LIMIT_BYTES, flags=flags)


def _dot(a, b):
    return jnp.dot(a, b, preferred_element_type=F32)


def _dot_nt(a, b):
    return lax.dot_general(a, b, (((1,), (1,)), ((), ())), preferred_element_type=F32)


def _resident(shape):
    zeros = (0,) * len(shape)
    return pl.BlockSpec(shape, lambda *_: zeros, pipeline_mode=pl.Buffered(1))


def _ada_kernel(c_ref, w_ref, b_ref, o_ref):
    cv = c_ref[...]
    o_ref[...] = _dot(cv * jax.nn.sigmoid(cv), w_ref[...]) + b_ref[...]


def ada_mod(cvec, w_ada, b_ada, *, tn=1024):
    rows, d = cvec.shape
    n = w_ada.shape[1]
    return pl.pallas_call(
        _ada_kernel,
        grid=(n // tn,),
        in_specs=[pl.BlockSpec((rows, d), lambda j: (0, 0)),
                  pl.BlockSpec((d, tn), lambda j: (0, j)),
                  pl.BlockSpec((1, tn), lambda j: (0, j))],
        out_specs=pl.BlockSpec((rows, tn), lambda j: (0, j)),
        out_shape=jax.ShapeDtypeStruct((rows, n), F32),
        compiler_params=_params(("arbitrary",)),
        name="ada_mod",
    )(cvec, w_ada, b_ada.reshape(1, n))


def _rms(x):
    return x * lax.rsqrt(jnp.mean(x * x, axis=-1, keepdims=True) + EPS)


def _norm_mod(x, g, shift, scale):
    return _rms(x) * g * (1.0 + scale) + shift


ROW_SUB = 256


def _rope(xs, cos, sin_lo, sin_hi):
    half = HEAD_DIM // 4
    up = pltpu.roll(xs, HEAD_COLS - half, 1)
    dn = pltpu.roll(xs, half, 1)
    return xs * cos + up * sin_lo + dn * sin_hi


def _in_proj_kernel(x_ref, g_ref, sh_ref, sc_ref, w_ref, tab_ref, o_ref, h_ref, *,
                    rope_tiles, gate_tile0):
    j = pl.program_id(1)
    tm, tn = o_ref.shape
    n_sub = tm // ROW_SUB

    def run(first, kind):
        w = w_ref[...].astype(BF16)
        accs = {}

        def issue(idx):
            rows = slice(idx * ROW_SUB, (idx + 1) * ROW_SUB)
            if first:
                h = _norm_mod(x_ref[rows], g_ref[...], sh_ref[0], sc_ref[0]).astype(BF16)
                h_ref[rows] = h
            else:
                h = h_ref[rows]
            accs[idx] = _dot(h, w)

        issue(0)
        for idx in range(n_sub):
            if idx + 1 < n_sub:
                issue(idx + 1)
            acc = accs.pop(idx)
            rows = slice(idx * ROW_SUB, (idx + 1) * ROW_SUB)
            if kind == "rope":
                tab = tab_ref[0, rows]
                cos = tab[:, :HEAD_COLS]
                sin_lo = tab[:, HEAD_COLS:2 * HEAD_COLS]
                sin_hi = tab[:, 2 * HEAD_COLS:]
                for c in range(tn // HEAD_COLS):
                    cols = slice(c * HEAD_COLS, (c + 1) * HEAD_COLS)
                    o_ref[rows, cols] = _rope(acc[:, cols], cos, sin_lo, sin_hi).astype(o_ref.dtype)
            elif kind == "gate":
                o_ref[rows] = jax.nn.sigmoid(acc).astype(o_ref.dtype)
            else:
                o_ref[rows] = acc.astype(o_ref.dtype)

    if rope_tiles:
        pl.when(j == 0)(functools.partial(run, True, "rope"))
        pl.when((j > 0) & (j < rope_tiles))(functools.partial(run, False, "rope"))
        pl.when((j >= rope_tiles) & (j < gate_tile0))(functools.partial(run, False, "plain"))
        pl.when(j >= gate_tile0)(functools.partial(run, False, "gate"))
    else:
        pl.when(j == 0)(functools.partial(run, True, "plain"))
        pl.when(j > 0)(functools.partial(run, False, "plain"))


def in_proj(x2d, g, mod3, mod_row0, w, tabs, *, rows_per_mod, col_tile0, n_cols,
            rope_tiles, gate_tile0, tm, tn):
    m, d = x2d.shape
    tiles_per_mod = rows_per_mod // tm
    q_tiles = rope_tiles // 2 if rope_tiles else 1
    ltiles = tabs.shape[1] // tm if rope_tiles else 1

    def mod_map(which):
        return lambda i, j: ((mod_row0 + i // tiles_per_mod) * N_MOD + which, 0, 0)

    if rope_tiles:
        tab_spec = pl.BlockSpec((1, tm, 3 * HEAD_COLS),
                                lambda i, j: (jnp.minimum(j // q_tiles, 1), i % ltiles, 0))
    else:
        tab_spec = pl.BlockSpec((1, 8, 3 * HEAD_COLS), lambda i, j: (0, 0, 0))

    kern = functools.partial(_in_proj_kernel, rope_tiles=rope_tiles, gate_tile0=gate_tile0)
    return pl.pallas_call(
        kern,
        grid=(m // tm, n_cols // tn),
        in_specs=[pl.BlockSpec((tm, d), lambda i, j: (i, 0)),
                  pl.BlockSpec((1, d), lambda i, j: (0, 0)),
                  pl.BlockSpec((1, 1, d), mod_map(0)),
                  pl.BlockSpec((1, 1, d), mod_map(1)),
                  pl.BlockSpec((d, tn), lambda i, j: (0, j + col_tile0)),
                  tab_spec],
        out_specs=pl.BlockSpec((tm, tn), lambda i, j: (i, j)),
        out_shape=jax.ShapeDtypeStruct((m, n_cols), BF16),
        scratch_shapes=[pltpu.VMEM((tm, d), BF16)],
        compiler_params=_params(("arbitrary", "arbitrary")),
        name="in_proj",
    )(x2d, g.reshape(1, d), mod3, mod3, w, tabs)


def rope_tables(seq_len):
    pos = np.arange(seq_len)
    row = pos // GRID_W
    col = pos % GRID_W
    half = HEAD_DIM // 2
    inv_freq = ROPE_BASE ** (-np.arange(0, half, 2, dtype=np.float64) / half)

    def cs(p):
        ang = p[:, None].astype(np.float64) * inv_freq[None, :]
        ang = np.concatenate([ang, ang], axis=-1)
        return np.cos(ang), np.sin(ang)

    cr, sr = cs(row)
    cc, sc = cs(col)
    cos64 = np.concatenate([cr, cc], axis=-1)
    sin64 = np.concatenate([sr, sc], axis=-1)
    lower = (np.arange(HEAD_DIM) % half) < (half // 2)
    sin_lo64 = np.where(lower[None, :], -sin64, 0.0)
    sin_hi64 = np.where(lower[None, :], 0.0, sin64)
    tab = np.concatenate([np.tile(cos64, (1, 2)), np.tile(sin_lo64, (1, 2)),
                          np.tile(sin_hi64, (1, 2))], axis=-1)
    q_scale = HEAD_DIM ** -0.5 * math.log2(math.e)
    return jnp.asarray(np.stack([tab * q_scale, tab]), dtype=F32)


ATTN_KEY_CHUNK = 512
ATTN_LOOKAHEAD = 2
ATTN_UNSHIFTED_LIMIT = 80.0


def _component_sq_norms(x, first):
    return jnp.sum(jnp.where(first, x, 0.0), axis=-1, keepdims=True)


def _attn_kernel(lq1_ref, lk1_ref, lq2_ref, lk2_ref, gs_ref, q_ref, k_ref, kc_ref, v_ref, vc_ref,
                 *rest, n_cast):
    w_refs, o_ref, wbf_refs = rest[:n_cast], rest[n_cast], rest[n_cast + 1:-2]
    vt_ref, kn_ref = rest[-2:]
    n_lat = k_ref.shape[1]

    for w_ref, wbf_ref in zip(w_refs, wbf_refs):
        wbf_ref[...] = w_ref[...].astype(wbf_ref.dtype)

    @pl.when(pl.program_id(2) == 0)
    def _():
        for off in range(0, n_lat, ATTN_KEY_CHUNK):
            vt_ref[:, off:off + ATTN_KEY_CHUNK] = (
                v_ref[0, off:off + ATTN_KEY_CHUNK].astype(F32).T.astype(BF16))
        vt_ref[:, n_lat:] = vc_ref[0].astype(F32).T.astype(BF16)
        for comp in range(2):
            best = None
            for keys in (k_ref[0], kc_ref[0]):
                k32 = keys.astype(F32)
                first = lax.broadcasted_iota(jnp.int32, k32.shape, 1) < HEAD_DIM
                norms = _component_sq_norms(k32 * k32, first == (comp == 0))
                top = jnp.max(norms, axis=0, keepdims=True)
                best = top if best is None else jnp.maximum(best, top)
            kn_ref[comp] = jnp.broadcast_to(best, kn_ref.shape[1:])

    lam = (jnp.exp(jnp.sum(lq1_ref[...] * lk1_ref[...], axis=-1, keepdims=True))
           - jnp.exp(jnp.sum(lq2_ref[...] * lk2_ref[...], axis=-1, keepdims=True))
           + LAM_INIT)
    q = q_ref[0]
    lane = lax.broadcasted_iota(jnp.int32, q.shape, 1)
    qms = [jnp.where((lane < HEAD_DIM) == (comp == 0), q, jnp.zeros_like(q)) for comp in range(2)]
    chunks = ([(off, ATTN_KEY_CHUNK) for off in range(0, n_lat, ATTN_KEY_CHUNK)]
              + [(n_lat, kc_ref.shape[1])])
    items = [(off, width, comp) for off, width in chunks for comp in range(2)]
    ahead = 2 * ATTN_LOOKAHEAD

    q32 = q.astype(F32)
    bound_sq = None
    for comp in range(2):
        qn = jnp.max(_component_sq_norms(q32 * q32, (lane < HEAD_DIM) == (comp == 0)),
                     axis=0, keepdims=True)
        prod = qn * kn_ref[comp, 0:1, 0:1]
        bound_sq = prod if bound_sq is None else jnp.maximum(bound_sq, prod)
    small_scores = bound_sq[0, 0] <= ATTN_UNSHIFTED_LIMIT ** 2

    def run(shifted):
        scores = {}

        def issue(i):
            off, width, comp = items[i]
            keys = k_ref[0, off:off + width] if off < n_lat else kc_ref[0]
            scores[i] = _dot_nt(keys, qms[comp])

        for i in range(min(ahead, len(items))):
            issue(i)
        m = [None, None]
        d = [None, None]
        acc = [None, None]
        for i, (off, width, comp) in enumerate(items):
            if i + ahead < len(items):
                issue(i + ahead)
            s = scores.pop(i)
            vals_t = vt_ref[:, off:off + width]
            if not shifted:
                p = jnp.exp2(s)
                part = jnp.sum(p, axis=0, keepdims=True)
                pv = _dot(vals_t, p.astype(BF16))
                d[comp] = part if d[comp] is None else d[comp] + part
                acc[comp] = pv if acc[comp] is None else acc[comp] + pv
                continue
            col_max = jnp.max(s, axis=0, keepdims=True)
            if m[comp] is None:
                m[comp] = col_max
                p = jnp.exp2(s - col_max)
                d[comp] = jnp.sum(p, axis=0, keepdims=True)
                acc[comp] = _dot(vals_t, p.astype(BF16))
            else:
                m_new = jnp.maximum(m[comp], col_max)
                alpha = jnp.exp2(m[comp] - m_new)
                p = jnp.exp2(s - m_new)
                d[comp] = alpha * d[comp] + jnp.sum(p, axis=0, keepdims=True)
                acc[comp] = alpha * acc[comp] + _dot(vals_t, p.astype(BF16))
                m[comp] = m_new
        o_t = acc[0] * (1.0 / d[0]) - acc[1] * (lam / d[1])
        o = _rms(o_t.T) * gs_ref[...] * (1.0 - LAM_INIT)
        o_ref[0] = o.astype(o_ref.dtype)

    pl.when(small_scores)(functools.partial(run, False))
    pl.when(jnp.logical_not(small_scores))(functools.partial(run, True))


def diff_attn(p3, pc3, lam_vecs, g_subln, cast_weights, *, n_heads, tq):
    b, l, _ = p3.shape
    ctx = pc3.shape[1]
    h = n_heads
    nq = l // tq
    n_steps = b * h * nq
    vec = pl.BlockSpec((1, HEAD_DIM), lambda bi, hi, qi: (0, 0))

    def slab_spec(w):
        rows, cols = w.shape
        blk = max(BF16_ROW_TILE, rows // n_steps)
        last = rows // blk - 1
        return pl.BlockSpec(
            (blk, cols), lambda bi, hi, qi: (jnp.minimum((bi * h + hi) * nq + qi, last), 0))

    slabs = [slab_spec(w) for w in cast_weights]
    outs = pl.pallas_call(
        functools.partial(_attn_kernel, n_cast=len(cast_weights)),
        grid=(b, h, nq),
        in_specs=[vec, vec, vec, vec,
                  pl.BlockSpec((1, HEAD_COLS), lambda bi, hi, qi: (0, 0)),
                  pl.BlockSpec((1, tq, HEAD_COLS), lambda bi, hi, qi: (bi, qi, hi)),
                  pl.BlockSpec((1, l, HEAD_COLS), lambda bi, hi, qi: (bi, 0, h + hi)),
                  pl.BlockSpec((1, ctx, HEAD_COLS), lambda bi, hi, qi: (bi, 0, hi)),
                  pl.BlockSpec((1, l, HEAD_COLS), lambda bi, hi, qi: (bi, 0, 2 * h + hi)),
                  pl.BlockSpec((1, ctx, HEAD_COLS), lambda bi, hi, qi: (bi, 0, h + hi))] + slabs,
        out_specs=[pl.BlockSpec((1, tq, HEAD_COLS), lambda bi, hi, qi: (bi, qi, hi))] + slabs,
        out_shape=[jax.ShapeDtypeStruct((b, l, h * HEAD_COLS), BF16)]
                  + [jax.ShapeDtypeStruct(w.shape, BF16) for w in cast_weights],
        scratch_shapes=[pltpu.VMEM((HEAD_COLS, l + ctx), BF16),
                        pltpu.VMEM((2, 8, 128), F32)],
        compiler_params=_params(("arbitrary", "arbitrary", "arbitrary")),
        name="diff_attn",
    )(*lam_vecs, g_subln.reshape(1, HEAD_COLS), p3, p3, pc3, p3, pc3, *cast_weights)
    return outs[0], outs[1:]


FOURIER_N1 = 16


def _bf16_const(a):
    return jnp.asarray(a, dtype=F32).astype(BF16)


def dft_tables(n, scale):
    idx = (np.arange(n)[:, None] * np.arange(n)[None, :]) % n
    ang = 2.0 * np.pi * idx / n
    return np.cos(ang) * scale, np.sin(ang) * scale


def stage_a_tables(l, n1, sub, scale):
    n2 = l // n1
    j = np.arange(n2 // sub)[:, None, None, None]
    k1 = np.arange(n1)[None, :, None, None]
    a = np.arange(sub)[None, None, :, None]
    m1 = np.arange(n1)[None, None, None, :]
    ang = 2.0 * np.pi * ((k1 * (m1 * n2 + j * sub + a)) % l) / l
    eye = np.eye(sub)
    c = np.einsum("jkam,ab->jkamb", np.cos(ang) * scale, eye).reshape(n2 // sub, n1 * sub, n1 * sub)
    s = np.einsum("jkam,ab->jkamb", np.sin(ang) * scale, eye).reshape(n2 // sub, n1 * sub, n1 * sub)
    return np.concatenate([c, -s], axis=1), np.concatenate([-s, -c], axis=1)


def _fourier_a_kernel(cc_ref, sc_ref, ka_ref, kb_ref, u_ref, yr_ref, yi_ref, *, gd, sub):
    rows = ka_ref.shape[2]
    for t in range(ka_ref.shape[0]):
        inner = slice(t * sub, (t + 1) * sub)
        u = u_ref[0, :, inner].reshape(rows, u_ref.shape[3])
        groups = range(u.shape[1] // gd)
        uc = jnp.concatenate([_dot(u[:, g * gd:(g + 1) * gd], cc_ref[...]) for g in groups], axis=1)
        us = jnp.concatenate([_dot(u[:, g * gd:(g + 1) * gd], sc_ref[...]) for g in groups], axis=1)
        y = _dot(ka_ref[t], uc.astype(BF16)) + _dot(kb_ref[t], us.astype(BF16))
        block = (yr_ref.shape[1], sub, yr_ref.shape[3])
        yr_ref[0, :, inner] = y[:rows].astype(yr_ref.dtype).reshape(block)
        yi_ref[0, :, inner] = y[rows:].astype(yi_ref.dtype).reshape(block)


def fourier_a(p4, cos_c, sin_c, ka, kb, *, col_block, width, tiles_per_step):
    b, n1, n2, _ = p4.shape
    gd = cos_c.shape[0]
    tiles, rows2, rows = ka.shape
    sub = rows // n1
    tps = tiles_per_step
    tab = pl.BlockSpec((tps, rows2, rows), lambda bi, j: (j, 0, 0))
    out_spec = pl.BlockSpec((1, n1, tps * sub, width), lambda bi, j: (bi, 0, j, 0))
    out = jax.ShapeDtypeStruct((b, n1, n2, width), BF16)
    return pl.pallas_call(
        functools.partial(_fourier_a_kernel, gd=gd, sub=sub),
        grid=(b, tiles // tps),
        in_specs=[_resident((gd, gd)), _resident((gd, gd)), tab, tab,
                  pl.BlockSpec((1, n1, tps * sub, width), lambda bi, j: (bi, 0, j, col_block))],
        out_specs=[out_spec, out_spec],
        out_shape=[out, out],
        compiler_params=_params(("arbitrary", "arbitrary")),
        name="fourier_a",
    )(cos_c, sin_c, ka, kb, p4)


def _fourier_c_kernel(c_ref, s_ref, yr_ref, yi_ref, o_ref):
    for t in range(yr_ref.shape[1]):
        o = _dot(c_ref[...], yr_ref[0, t]) + _dot(s_ref[...], yi_ref[0, t])
        o_ref[0, t] = o.astype(o_ref.dtype)


def fourier_c(yr4, yi4, cos2, sin2, *, k1_per_step):
    b, n1, n2, width = yr4.shape
    data = pl.BlockSpec((1, k1_per_step, n2, width), lambda bi, k: (bi, k, 0, 0))
    return pl.pallas_call(
        _fourier_c_kernel,
        grid=(b, n1 // k1_per_step),
        in_specs=[_resident((n2, n2)), _resident((n2, n2)), data, data],
        out_specs=data,
        out_shape=jax.ShapeDtypeStruct((b, n1, n2, width), BF16),
        compiler_params=_params(("arbitrary", "arbitrary")),
        name="fourier_c",
    )(cos2, sin2, yr4, yi4)


def position_order_matrix(n1, sub):
    p = np.zeros((sub * n1, n1 * sub))
    for k1 in range(n1):
        for k2 in range(sub):
            p[k2 * n1 + k1, k1 * sub + k2] = 1.0
    return p


def _merge_kernel(a_ref, f_ref, ga_ref, gf_ref, x_ref, gt_ref, pm_ref, wa_ref, wf_ref, wo_ref, o_ref):
    ya = _dot(a_ref[...], wa_ref[...])
    f = f_ref[0].reshape(pm_ref.shape[1], f_ref.shape[3])
    yf = _dot(_dot(pm_ref[...], f).astype(BF16), wf_ref[...])
    merged = ga_ref[...].astype(F32) * ya + gf_ref[...].astype(F32) * yf
    y = _dot(merged.astype(BF16), wo_ref[...])
    o_ref[...] = x_ref[...] + gt_ref[0] * y


def merge(attn2d, four4, pm, p2d, x2d, mod3, wa, wf, wo, *, rows_per_mod, gate_col0):
    m, d = x2d.shape
    dw = attn2d.shape[1]
    _, n1, _, fw = four4.shape
    tm = pm.shape[0]
    sub = tm // n1
    tiles_per_mod = rows_per_mod // tm
    ga_blk = gate_col0 // d
    return pl.pallas_call(
        _merge_kernel,
        grid=(m // tm,),
        in_specs=[pl.BlockSpec((tm, dw), lambda i: (i, 0)),
                  pl.BlockSpec((1, n1, sub, fw), lambda i: (i // tiles_per_mod, 0, i % tiles_per_mod, 0)),
                  pl.BlockSpec((tm, d), lambda i: (i, ga_blk)),
                  pl.BlockSpec((tm, d), lambda i: (i, ga_blk + 1)),
                  pl.BlockSpec((tm, d), lambda i: (i, 0)),
                  pl.BlockSpec((1, 1, d), lambda i: ((i // tiles_per_mod) * N_MOD + 2, 0, 0)),
                  _resident(pm.shape), _resident(wa.shape), _resident(wf.shape), _resident(wo.shape)],
        out_specs=pl.BlockSpec((tm, d), lambda i: (i, 0)),
        out_shape=jax.ShapeDtypeStruct((m, d), F32),
        compiler_params=_params(("arbitrary",)),
        name="merge",
    )(attn2d, four4, p2d, p2d, x2d, mod3, pm, wa, wf, wo)


def _mlp_kernel(x_ref, g_ref, sh_ref, sc_ref, gt_ref, gf_ref, w1_ref, w2_ref, o_ref, h_ref, acc_ref):
    j = pl.program_id(1)
    last = pl.num_programs(1) - 1
    n_sub = x_ref.shape[0] // ROW_SUB

    def run(first, final):
        zs = {}

        def issue(idx):
            rows = slice(idx * ROW_SUB, (idx + 1) * ROW_SUB)
            if first:
                h = _norm_mod(x_ref[rows], g_ref[...], sh_ref[0], sc_ref[0]).astype(BF16)
                h_ref[rows] = h
            else:
                h = h_ref[rows]
            zs[idx] = jnp.maximum(_dot(h, w1_ref[...]), 0.0)

        issue(0)
        for idx in range(n_sub):
            if idx + 1 < n_sub:
                issue(idx + 1)
            z = zs.pop(idx)
            rows = slice(idx * ROW_SUB, (idx + 1) * ROW_SUB)
            total = _dot((z * z).astype(BF16), w2_ref[...])
            if not first:
                total = acc_ref[rows] + total
            if final:
                y = x_ref[rows] + gt_ref[0] * total
                o_ref[rows] = _rms(y) * gf_ref[...]
            else:
                acc_ref[rows] = total

    pl.when(j == 0)(functools.partial(run, True, False))
    pl.when((j > 0) & (j < last))(functools.partial(run, False, False))
    pl.when(j == last)(functools.partial(run, False, True))


def mlp(x2d, g2, mod3, g_final, w1, w2, *, rows_per_mod, tm, tf):
    m, d = x2d.shape
    f = w1.shape[1]
    assert f // tf >= 2 and tm % ROW_SUB == 0
    tiles_per_mod = rows_per_mod // tm

    def mod_map(which):
        return lambda i, j: ((i // tiles_per_mod) * N_MOD + which, 0, 0)

    return pl.pallas_call(
        _mlp_kernel,
        grid=(m // tm, f // tf),
        in_specs=[pl.BlockSpec((tm, d), lambda i, j: (i, 0)),
                  pl.BlockSpec((1, d), lambda i, j: (0, 0)),
                  pl.BlockSpec((1, 1, d), mod_map(3)),
                  pl.BlockSpec((1, 1, d), mod_map(4)),
                  pl.BlockSpec((1, 1, d), mod_map(5)),
                  pl.BlockSpec((1, d), lambda i, j: (0, 0)),
                  pl.BlockSpec((d, tf), lambda i, j: (0, j)),
                  pl.BlockSpec((tf, d), lambda i, j: (j, 0))],
        out_specs=pl.BlockSpec((tm, d), lambda i, j: (i, 0), pipeline_mode=pl.Buffered(1)),
        out_shape=jax.ShapeDtypeStruct((m, d), F32),
        scratch_shapes=[pltpu.VMEM((tm, d), BF16), pltpu.VMEM((tm, d), F32)],
        compiler_params=_params(("arbitrary", "arbitrary")),
        name="mlp",
    )(x2d, g2.reshape(1, d), mod3, mod3, mod3, g_final.reshape(1, d), w1, w2)


def kernel(x, c, ctx, c_ctx, w_ada, b_ada, g_norm1, w_in, lam_q1, lam_k1, lam_q2, lam_k2, g_subln,
           w_attn_br, w_four_br, w_out, g_norm2, w_mlp_in, w_mlp_out, g_final):
    b, l, d = x.shape
    n_ctx = ctx.shape[1]
    assert w_ada.shape[0] == 1, "single-layer block"
    dw = w_attn_br.shape[1]
    fw = w_four_br.shape[1]
    n_heads = dw // HEAD_COLS
    gd = fw // N_FOURIER_GROUPS
    in_cols = w_in.shape[2]
    gate_col0 = 3 * dw + fw
    assert in_cols == gate_col0 + 2 * d

    cvec = jnp.zeros((8, d), F32).at[:b].set(c).at[b].set(c_ctx)
    mod = ada_mod(cvec, w_ada[0], b_ada[0])
    mod3 = mod[:b + 1].reshape((b + 1) * N_MOD, 1, d)

    tabs = rope_tables(l)
    tm, tn = 1024, 1024
    x2d = x.reshape(b * l, d)
    p2d = in_proj(x2d, g_norm1[0], mod3, 0, w_in[0], tabs, rows_per_mod=l, col_tile0=0,
                  n_cols=in_cols, rope_tiles=2 * dw // tn, gate_tile0=gate_col0 // tn, tm=tm, tn=tn)
    pc2d = in_proj(ctx.reshape(b * n_ctx, d), g_norm1[0], mod3, b, w_in[0], tabs,
                   rows_per_mod=b * n_ctx, col_tile0=dw // tn, n_cols=2 * dw,
                   rope_tiles=0, gate_tile0=0, tm=b * n_ctx, tn=tn)

    lam_vecs = [v[0].astype(F32).reshape(1, HEAD_DIM) for v in (lam_q1, lam_k1, lam_q2, lam_k2)]
    attn, (wa_bf, wf_bf, wo_bf, w1_bf, w2_bf) = diff_attn(
        p2d.reshape(b, l, in_cols), pc2d.reshape(b, n_ctx, 2 * dw), lam_vecs, g_subln[0],
        [w_attn_br[0], w_four_br[0], w_out[0], w_mlp_in[0], w_mlp_out[0]], n_heads=n_heads, tq=1024)

    n1 = FOURIER_N1
    n2 = l // n1
    sub = FOURIER_N1
    assert n1 * n2 == l and n2 % sub == 0
    cos_c, sin_c = (_bf16_const(t) for t in dft_tables(gd, gd ** -0.5))
    ka, kb = (_bf16_const(t) for t in stage_a_tables(l, n1, sub, l ** -0.5))
    cos2, sin2 = (_bf16_const(t) for t in dft_tables(n2, 1.0))
    yr, yi = fourier_a(p2d.reshape(b, n1, n2, in_cols), cos_c, sin_c, ka, kb,
                       col_block=3 * dw // fw, width=fw, tiles_per_step=4)
    four4 = fourier_c(yr, yi, cos2, sin2, k1_per_step=4)

    x_mid = merge(attn.reshape(b * l, dw), four4, _bf16_const(position_order_matrix(n1, sub)), p2d, x2d,
                  mod3, wa_bf, wf_bf, wo_bf, rows_per_mod=l, gate_col0=gate_col0)
    out = mlp(x_mid, g_norm2[0], mod3, g_final, w1_bf, w2_bf,
              rows_per_mod=l, tm=1024, tf=1024)
    return out.reshape(b, l, d)
```

```python
import functools
import math

import jax
import jax.numpy as jnp
import numpy as np
from jax import lax
from jax.experimental import pallas as pl
from jax.experimental.pallas import tpu as pltpu

GRID_W = 64
HEAD_DIM = 64
HEAD_COLS = 2 * HEAD_DIM
N_FOURIER_GROUPS = 4
N_MOD = 6
ROPE_BASE = 10000.0
EPS = 1e-6
LAM_INIT = 0.8 - 0.6 * math.exp(-0.3 * 0)

V7X_VMEM_LIMIT_BYTES = 58 * 1024 * 1024
BF16 = jnp.bfloat16
BF16_ROW_TILE = 16
F32 = jnp.float32


def _params(semantics, flags=None):
    return pltpu.CompilerParams(dimension_semantics=semantics,
                                vmem_limit_bytes=V7X_VMEM_LIMIT_BYTES, flags=flags)


def _dot(a, b):
    return jnp.dot(a, b, preferred_element_type=F32)


def _dot_nt(a, b):
    return lax.dot_general(a, b, (((1,), (1,)), ((), ())), preferred_element_type=F32)


def _resident(shape):
    zeros = (0,) * len(shape)
    return pl.BlockSpec(shape, lambda *_: zeros, pipeline_mode=pl.Buffered(1))


def _ada_kernel(c_ref, w_ref, b_ref, o_ref):
    cv = c_ref[...]
    o_ref[...] = _dot(cv * jax.nn.sigmoid(cv), w_ref[...]) + b_ref[...]


def ada_mod(cvec, w_ada, b_ada, *, tn=1024):
    rows, d = cvec.shape
    n = w_ada.shape[1]
    return pl.pallas_call(
        _ada_kernel,
        grid=(n // tn,),
        in_specs=[pl.BlockSpec((rows, d), lambda j: (0, 0)),
                  pl.BlockSpec((d, tn), lambda j: (0, j)),
                  pl.BlockSpec((1, tn), lambda j: (0, j))],
        out_specs=pl.BlockSpec((rows, tn), lambda j: (0, j)),
        out_shape=jax.ShapeDtypeStruct((rows, n), F32),
        compiler_params=_params(("arbitrary",)),
        name="ada_mod",
    )(cvec, w_ada, b_ada.reshape(1, n))


def _rms(x):
    return x * lax.rsqrt(jnp.mean(x * x, axis=-1, keepdims=True) + EPS)


def _norm_mod(x, g, shift, scale):
    return _rms(x) * g * (1.0 + scale) + shift


ROW_SUB = 256


def _rope(xs, cos, sin_lo, sin_hi):
    half = HEAD_DIM // 4
    up = pltpu.roll(xs, HEAD_COLS - half, 1)
    dn = pltpu.roll(xs, half, 1)
    return xs * cos + up * sin_lo + dn * sin_hi


def _in_proj_kernel(x_ref, g_ref, sh_ref, sc_ref, w_ref, tab_ref, o_ref, h_ref, *,
                    rope_tiles, gate_tile0):
    j = pl.program_id(1)
    tm, tn = o_ref.shape
    n_sub = tm // ROW_SUB

    def run(first, kind):
        w = w_ref[...].astype(BF16)
        accs = {}

        def issue(idx):
            rows = slice(idx * ROW_SUB, (idx + 1) * ROW_SUB)
            if first:
                h = _norm_mod(x_ref[rows], g_ref[...], sh_ref[0], sc_ref[0]).astype(BF16)
                h_ref[rows] = h
            else:
                h = h_ref[rows]
            accs[idx] = _dot(h, w)

        issue(0)
        for idx in range(n_sub):
            if idx + 1 < n_sub:
                issue(idx + 1)
            acc = accs.pop(idx)
            rows = slice(idx * ROW_SUB, (idx + 1) * ROW_SUB)
            if kind == "rope":
                tab = tab_ref[0, rows]
                cos = tab[:, :HEAD_COLS]
                sin_lo = tab[:, HEAD_COLS:2 * HEAD_COLS]
                sin_hi = tab[:, 2 * HEAD_COLS:]
                for c in range(tn // HEAD_COLS):
                    cols = slice(c * HEAD_COLS, (c + 1) * HEAD_COLS)
                    o_ref[rows, cols] = _rope(acc[:, cols], cos, sin_lo, sin_hi).astype(o_ref.dtype)
            elif kind == "gate":
                o_ref[rows] = jax.nn.sigmoid(acc).astype(o_ref.dtype)
            else:
                o_ref[rows] = acc.astype(o_ref.dtype)

    if rope_tiles:
        pl.when(j == 0)(functools.partial(run, True, "rope"))
        pl.when((j > 0) & (j < rope_tiles))(functools.partial(run, False, "rope"))
        pl.when((j >= rope_tiles) & (j < gate_tile0))(functools.partial(run, False, "plain"))
        pl.when(j >= gate_tile0)(functools.partial(run, False, "gate"))
    else:
        pl.when(j == 0)(functools.partial(run, True, "plain"))
        pl.when(j > 0)(functools.partial(run, False, "plain"))


def in_proj(x2d, g, mod3, mod_row0, w, tabs, *, rows_per_mod, col_tile0, n_cols,
            rope_tiles, gate_tile0, tm, tn):
    m, d = x2d.shape
    tiles_per_mod = rows_per_mod // tm
    q_tiles = rope_tiles // 2 if rope_tiles else 1
    ltiles = tabs.shape[1] // tm if rope_tiles else 1

    def mod_map(which):
        return lambda i, j: ((mod_row0 + i // tiles_per_mod) * N_MOD + which, 0, 0)

    if rope_tiles:
        tab_spec = pl.BlockSpec((1, tm, 3 * HEAD_COLS),
                                lambda i, j: (jnp.minimum(j // q_tiles, 1), i % ltiles, 0))
    else:
        tab_spec = pl.BlockSpec((1, 8, 3 * HEAD_COLS), lambda i, j: (0, 0, 0))

    kern = functools.partial(_in_proj_kernel, rope_tiles=rope_tiles, gate_tile0=gate_tile0)
    return pl.pallas_call(
        kern,
        grid=(m // tm, n_cols // tn),
        in_specs=[pl.BlockSpec((tm, d), lambda i, j: (i, 0)),
                  pl.BlockSpec((1, d), lambda i, j: (0, 0)),
                  pl.BlockSpec((1, 1, d), mod_map(0)),
                  pl.BlockSpec((1, 1, d), mod_map(1)),
                  pl.BlockSpec((d, tn), lambda i, j: (0, j + col_tile0)),
                  tab_spec],
        out_specs=pl.BlockSpec((tm, tn), lambda i, j: (i, j)),
        out_shape=jax.ShapeDtypeStruct((m, n_cols), BF16),
        scratch_shapes=[pltpu.VMEM((tm, d), BF16)],
        compiler_params=_params(("arbitrary", "arbitrary")),
        name="in_proj",
    )(x2d, g.reshape(1, d), mod3, mod3, w, tabs)


def rope_tables(seq_len):
    pos = np.arange(seq_len)
    row = pos // GRID_W
    col = pos % GRID_W
    half = HEAD_DIM // 2
    inv_freq = ROPE_BASE ** (-np.arange(0, half, 2, dtype=np.float64) / half)

    def cs(p):
        ang = p[:, None].astype(np.float64) * inv_freq[None, :]
        ang = np.concatenate([ang, ang], axis=-1)
        return np.cos(ang), np.sin(ang)

    cr, sr = cs(row)
    cc, sc = cs(col)
    cos64 = np.concatenate([cr, cc], axis=-1)
    sin64 = np.concatenate([sr, sc], axis=-1)
    lower = (np.arange(HEAD_DIM) % half) < (half // 2)
    sin_lo64 = np.where(lower[None, :], -sin64, 0.0)
    sin_hi64 = np.where(lower[None, :], 0.0, sin64)
    tab = np.concatenate([np.tile(cos64, (1, 2)), np.tile(sin_lo64, (1, 2)),
                          np.tile(sin_hi64, (1, 2))], axis=-1)
    q_scale = HEAD_DIM ** -0.5 * math.log2(math.e)
    return jnp.asarray(np.stack([tab * q_scale, tab]), dtype=F32)


ATTN_KEY_CHUNK = 512
ATTN_LOOKAHEAD = 2
ATTN_UNSHIFTED_LIMIT = 80.0
ATTN_UNSHIFTED_VALUE_LIMIT = 2.0 ** 30


def _component_sq_norms(x, first):
    return jnp.sum(jnp.where(first, x, 0.0), axis=-1, keepdims=True)


def _attn_kernel(lq1_ref, lk1_ref, lq2_ref, lk2_ref, gs_ref, q_ref, k_ref, kc_ref, v_ref, vc_ref,
                 *rest, n_cast):
    w_refs, o_ref, wbf_refs = rest[:n_cast], rest[n_cast], rest[n_cast + 1:-2]
    vt_ref, kn_ref = rest[-2:]
    n_lat = k_ref.shape[1]

    for w_ref, wbf_ref in zip(w_refs, wbf_refs):
        wbf_ref[...] = w_ref[...].astype(wbf_ref.dtype)

    @pl.when(pl.program_id(2) == 0)
    def _():
        for off in range(0, n_lat, ATTN_KEY_CHUNK):
            vt_ref[:, off:off + ATTN_KEY_CHUNK] = (
                v_ref[0, off:off + ATTN_KEY_CHUNK].astype(F32).T.astype(BF16))
        vt_ref[:, n_lat:] = vc_ref[0].astype(F32).T.astype(BF16)
        for comp in range(2):
            best = None
            for keys in (k_ref[0], kc_ref[0]):
                k32 = keys.astype(F32)
                first = lax.broadcasted_iota(jnp.int32, k32.shape, 1) < HEAD_DIM
                norms = _component_sq_norms(k32 * k32, first == (comp == 0))
                top = jnp.max(norms, axis=0, keepdims=True)
                best = top if best is None else jnp.maximum(best, top)
            kn_ref[comp] = jnp.broadcast_to(best, kn_ref.shape[1:])
        v_top = jnp.maximum(jnp.max(jnp.abs(v_ref[0].astype(F32)), axis=0, keepdims=True),
                            jnp.max(jnp.abs(vc_ref[0].astype(F32)), axis=0, keepdims=True))
        kn_ref[2] = jnp.broadcast_to(jnp.max(v_top, axis=-1, keepdims=True), kn_ref.shape[1:])

    lam = (jnp.exp(jnp.sum(lq1_ref[...] * lk1_ref[...], axis=-1, keepdims=True))
           - jnp.exp(jnp.sum(lq2_ref[...] * lk2_ref[...], axis=-1, keepdims=True))
           + LAM_INIT)
    q = q_ref[0]
    lane = lax.broadcasted_iota(jnp.int32, q.shape, 1)
    qms = [jnp.where((lane < HEAD_DIM) == (comp == 0), q, jnp.zeros_like(q)) for comp in range(2)]
    chunks = ([(off, ATTN_KEY_CHUNK) for off in range(0, n_lat, ATTN_KEY_CHUNK)]
              + [(n_lat, kc_ref.shape[1])])
    items = [(off, width, comp) for off, width in chunks for comp in range(2)]
    ahead = 2 * ATTN_LOOKAHEAD

    q32 = q.astype(F32)
    bound_sq = None
    for comp in range(2):
        qn = jnp.max(_component_sq_norms(q32 * q32, (lane < HEAD_DIM) == (comp == 0)),
                     axis=0, keepdims=True)
        prod = qn * kn_ref[comp, 0:1, 0:1]
        bound_sq = prod if bound_sq is None else jnp.maximum(bound_sq, prod)
    small_scores = ((bound_sq[0, 0] <= ATTN_UNSHIFTED_LIMIT ** 2)
                    & (kn_ref[2, 0:1, 0:1][0, 0] <= ATTN_UNSHIFTED_VALUE_LIMIT))

    def run(shifted):
        scores = {}

        def issue(i):
            off, width, comp = items[i]
            keys = k_ref[0, off:off + width] if off < n_lat else kc_ref[0]
            scores[i] = _dot_nt(keys, qms[comp])

        for i in range(min(ahead, len(items))):
            issue(i)
        m = [None, None]
        d = [None, None]
        acc = [None, None]
        for i, (off, width, comp) in enumerate(items):
            if i + ahead < len(items):
                issue(i + ahead)
            s = scores.pop(i)
            vals_t = vt_ref[:, off:off + width]
            if not shifted:
                p = jnp.exp2(s)
                part = jnp.sum(p, axis=0, keepdims=True)
                pv = _dot(vals_t, p.astype(BF16))
                d[comp] = part if d[comp] is None else d[comp] + part
                acc[comp] = pv if acc[comp] is None else acc[comp] + pv
                continue
            col_max = jnp.max(s, axis=0, keepdims=True)
            if m[comp] is None:
                m[comp] = col_max
                p = jnp.exp2(s - col_max)
                d[comp] = jnp.sum(p, axis=0, keepdims=True)
                acc[comp] = _dot(vals_t, p.astype(BF16))
            else:
                m_new = jnp.maximum(m[comp], col_max)
                alpha = jnp.exp2(m[comp] - m_new)
                p = jnp.exp2(s - m_new)
                d[comp] = alpha * d[comp] + jnp.sum(p, axis=0, keepdims=True)
                acc[comp] = alpha * acc[comp] + _dot(vals_t, p.astype(BF16))
                m[comp] = m_new
        o_t = acc[0] * (1.0 / d[0]) - acc[1] * (lam / d[1])
        o = _rms(o_t.T) * gs_ref[...] * (1.0 - LAM_INIT)
        o_ref[0] = o.astype(o_ref.dtype)

    pl.when(small_scores)(functools.partial(run, False))
    pl.when(jnp.logical_not(small_scores))(functools.partial(run, True))


def diff_attn(p3, pc3, lam_vecs, g_subln, cast_weights, *, n_heads, tq):
    b, l, _ = p3.shape
    ctx = pc3.shape[1]
    h = n_heads
    nq = l // tq
    n_steps = b * h * nq
    vec = pl.BlockSpec((1, HEAD_DIM), lambda bi, hi, qi: (0, 0))

    def slab_spec(w):
        rows, cols = w.shape
        blk = max(BF16_ROW_TILE, rows // n_steps)
        last = rows // blk - 1
        return pl.BlockSpec(
            (blk, cols), lambda bi, hi, qi: (jnp.minimum((bi * h + hi) * nq + qi, last), 0))

    slabs = [slab_spec(w) for w in cast_weights]
    outs = pl.pallas_call(
        functools.partial(_attn_kernel, n_cast=len(cast_weights)),
        grid=(b, h, nq),
        in_specs=[vec, vec, vec, vec,
                  pl.BlockSpec((1, HEAD_COLS), lambda bi, hi, qi: (0, 0)),
                  pl.BlockSpec((1, tq, HEAD_COLS), lambda bi, hi, qi: (bi, qi, hi)),
                  pl.BlockSpec((1, l, HEAD_COLS), lambda bi, hi, qi: (bi, 0, h + hi)),
                  pl.BlockSpec((1, ctx, HEAD_COLS), lambda bi, hi, qi: (bi, 0, hi)),
                  pl.BlockSpec((1, l, HEAD_COLS), lambda bi, hi, qi: (bi, 0, 2 * h + hi)),
                  pl.BlockSpec((1, ctx, HEAD_COLS), lambda bi, hi, qi: (bi, 0, h + hi))] + slabs,
        out_specs=[pl.BlockSpec((1, tq, HEAD_COLS), lambda bi, hi, qi: (bi, qi, hi))] + slabs,
        out_shape=[jax.ShapeDtypeStruct((b, l, h * HEAD_COLS), BF16)]
                  + [jax.ShapeDtypeStruct(w.shape, BF16) for w in cast_weights],
        scratch_shapes=[pltpu.VMEM((HEAD_COLS, l + ctx), BF16),
                        pltpu.VMEM((3, 8, 128), F32)],
        compiler_params=_params(("arbitrary", "arbitrary", "arbitrary")),
        name="diff_attn",
    )(*lam_vecs, g_subln.reshape(1, HEAD_COLS), p3, p3, pc3, p3, pc3, *cast_weights)
    return outs[0], outs[1:]


FOURIER_N1 = 16


def _bf16_const(a):
    return jnp.asarray(a, dtype=F32).astype(BF16)


def dft_tables(n, scale):
    idx = (np.arange(n)[:, None] * np.arange(n)[None, :]) % n
    ang = 2.0 * np.pi * idx / n
    return np.cos(ang) * scale, np.sin(ang) * scale


def stage_a_tables(l, n1, sub, scale):
    n2 = l // n1
    j = np.arange(n2 // sub)[:, None, None, None]
    k1 = np.arange(n1)[None, :, None, None]
    a = np.arange(sub)[None, None, :, None]
    m1 = np.arange(n1)[None, None, None, :]
    ang = 2.0 * np.pi * ((k1 * (m1 * n2 + j * sub + a)) % l) / l
    eye = np.eye(sub)
    c = np.einsum("jkam,ab->jkamb", np.cos(ang) * scale, eye).reshape(n2 // sub, n1 * sub, n1 * sub)
    s = np.einsum("jkam,ab->jkamb", np.sin(ang) * scale, eye).reshape(n2 // sub, n1 * sub, n1 * sub)
    return np.concatenate([c, -s], axis=1), np.concatenate([-s, -c], axis=1)


def _fourier_a_kernel(cc_ref, sc_ref, ka_ref, kb_ref, u_ref, yr_ref, yi_ref, *, gd, sub):
    rows = ka_ref.shape[2]
    for t in range(ka_ref.shape[0]):
        inner = slice(t * sub, (t + 1) * sub)
        u = u_ref[0, :, inner].reshape(rows, u_ref.shape[3])
        groups = range(u.shape[1] // gd)
        uc = jnp.concatenate([_dot(u[:, g * gd:(g + 1) * gd], cc_ref[...]) for g in groups], axis=1)
        us = jnp.concatenate([_dot(u[:, g * gd:(g + 1) * gd], sc_ref[...]) for g in groups], axis=1)
        y = _dot(ka_ref[t], uc.astype(BF16)) + _dot(kb_ref[t], us.astype(BF16))
        block = (yr_ref.shape[1], sub, yr_ref.shape[3])
        yr_ref[0, :, inner] = y[:rows].astype(yr_ref.dtype).reshape(block)
        yi_ref[0, :, inner] = y[rows:].astype(yi_ref.dtype).reshape(block)


def fourier_a(p4, cos_c, sin_c, ka, kb, *, col_block, width, tiles_per_step):
    b, n1, n2, _ = p4.shape
    gd = cos_c.shape[0]
    tiles, rows2, rows = ka.shape
    sub = rows // n1
    tps = tiles_per_step
    tab = pl.BlockSpec((tps, rows2, rows), lambda bi, j: (j, 0, 0))
    out_spec = pl.BlockSpec((1, n1, tps * sub, width), lambda bi, j: (bi, 0, j, 0))
    out = jax.ShapeDtypeStruct((b, n1, n2, width), BF16)
    return pl.pallas_call(
        functools.partial(_fourier_a_kernel, gd=gd, sub=sub),
        grid=(b, tiles // tps),
        in_specs=[_resident((gd, gd)), _resident((gd, gd)), tab, tab,
                  pl.BlockSpec((1, n1, tps * sub, width), lambda bi, j: (bi, 0, j, col_block))],
        out_specs=[out_spec, out_spec],
        out_shape=[out, out],
        compiler_params=_params(("arbitrary", "arbitrary")),
        name="fourier_a",
    )(cos_c, sin_c, ka, kb, p4)


def _fourier_c_kernel(c_ref, s_ref, yr_ref, yi_ref, o_ref):
    for t in range(yr_ref.shape[1]):
        o = _dot(c_ref[...], yr_ref[0, t]) + _dot(s_ref[...], yi_ref[0, t])
        o_ref[0, t] = o.astype(o_ref.dtype)


def fourier_c(yr4, yi4, cos2, sin2, *, k1_per_step):
    b, n1, n2, width = yr4.shape
    data = pl.BlockSpec((1, k1_per_step, n2, width), lambda bi, k: (bi, k, 0, 0))
    return pl.pallas_call(
        _fourier_c_kernel,
        grid=(b, n1 // k1_per_step),
        in_specs=[_resident((n2, n2)), _resident((n2, n2)), data, data],
        out_specs=data,
        out_shape=jax.ShapeDtypeStruct((b, n1, n2, width), BF16),
        compiler_params=_params(("arbitrary", "arbitrary")),
        name="fourier_c",
    )(cos2, sin2, yr4, yi4)


def position_order_matrix(n1, sub):
    p = np.zeros((sub * n1, n1 * sub))
    for k1 in range(n1):
        for k2 in range(sub):
            p[k2 * n1 + k1, k1 * sub + k2] = 1.0
    return p


def _merge_kernel(a_ref, f_ref, ga_ref, gf_ref, x_ref, gt_ref, pm_ref, wa_ref, wf_ref, wo_ref, o_ref):
    ya = _dot(a_ref[...], wa_ref[...])
    f = f_ref[0].reshape(pm_ref.shape[1], f_ref.shape[3])
    yf = _dot(_dot(pm_ref[...], f).astype(BF16), wf_ref[...])
    merged = ga_ref[...].astype(F32) * ya + gf_ref[...].astype(F32) * yf
    y = _dot(merged.astype(BF16), wo_ref[...])
    o_ref[...] = x_ref[...] + gt_ref[0] * y


def merge(attn2d, four4, pm, p2d, x2d, mod3, wa, wf, wo, *, rows_per_mod, gate_col0):
    m, d = x2d.shape
    dw = attn2d.shape[1]
    _, n1, _, fw = four4.shape
    tm = pm.shape[0]
    sub = tm // n1
    tiles_per_mod = rows_per_mod // tm
    ga_blk = gate_col0 // d
    return pl.pallas_call(
        _merge_kernel,
        grid=(m // tm,),
        in_specs=[pl.BlockSpec((tm, dw), lambda i: (i, 0)),
                  pl.BlockSpec((1, n1, sub, fw), lambda i: (i // tiles_per_mod, 0, i % tiles_per_mod, 0)),
                  pl.BlockSpec((tm, d), lambda i: (i, ga_blk)),
                  pl.BlockSpec((tm, d), lambda i: (i, ga_blk + 1)),
                  pl.BlockSpec((tm, d), lambda i: (i, 0)),
                  pl.BlockSpec((1, 1, d), lambda i: ((i // tiles_per_mod) * N_MOD + 2, 0, 0)),
                  _resident(pm.shape), _resident(wa.shape), _resident(wf.shape), _resident(wo.shape)],
        out_specs=pl.BlockSpec((tm, d), lambda i: (i, 0)),
        out_shape=jax.ShapeDtypeStruct((m, d), F32),
        compiler_params=_params(("arbitrary",)),
        name="merge",
    )(attn2d, four4, p2d, p2d, x2d, mod3, pm, wa, wf, wo)


def _mlp_kernel(x_ref, g_ref, sh_ref, sc_ref, gt_ref, gf_ref, w1_ref, w2_ref, o_ref, h_ref, acc_ref):
    j = pl.program_id(1)
    last = pl.num_programs(1) - 1
    n_sub = x_ref.shape[0] // ROW_SUB

    def run(first, final):
        zs = {}

        def issue(idx):
            rows = slice(idx * ROW_SUB, (idx + 1) * ROW_SUB)
            if first:
                h = _norm_mod(x_ref[rows], g_ref[...], sh_ref[0], sc_ref[0]).astype(BF16)
                h_ref[rows] = h
            else:
                h = h_ref[rows]
            zs[idx] = jnp.maximum(_dot(h, w1_ref[...]), 0.0)

        issue(0)
        for idx in range(n_sub):
            if idx + 1 < n_sub:
                issue(idx + 1)
            z = zs.pop(idx)
            rows = slice(idx * ROW_SUB, (idx + 1) * ROW_SUB)
            total = _dot((z * z).astype(BF16), w2_ref[...])
            if not first:
                total = acc_ref[rows] + total
            if final:
                y = x_ref[rows] + gt_ref[0] * total
                o_ref[rows] = _rms(y) * gf_ref[...]
            else:
                acc_ref[rows] = total

    pl.when(j == 0)(functools.partial(run, True, False))
    pl.when((j > 0) & (j < last))(functools.partial(run, False, False))
    pl.when(j == last)(functools.partial(run, False, True))


def mlp(x2d, g2, mod3, g_final, w1, w2, *, rows_per_mod, tm, tf):
    m, d = x2d.shape
    f = w1.shape[1]
    assert f // tf >= 2 and tm % ROW_SUB == 0
    tiles_per_mod = rows_per_mod // tm

    def mod_map(which):
        return lambda i, j: ((i // tiles_per_mod) * N_MOD + which, 0, 0)

    return pl.pallas_call(
        _mlp_kernel,
        grid=(m // tm, f // tf),
        in_specs=[pl.BlockSpec((tm, d), lambda i, j: (i, 0)),
                  pl.BlockSpec((1, d), lambda i, j: (0, 0)),
                  pl.BlockSpec((1, 1, d), mod_map(3)),
                  pl.BlockSpec((1, 1, d), mod_map(4)),
                  pl.BlockSpec((1, 1, d), mod_map(5)),
                  pl.BlockSpec((1, d), lambda i, j: (0, 0)),
                  pl.BlockSpec((d, tf), lambda i, j: (0, j)),
                  pl.BlockSpec((tf, d), lambda i, j: (j, 0))],
        out_specs=pl.BlockSpec((tm, d), lambda i, j: (i, 0), pipeline_mode=pl.Buffered(1)),
        out_shape=jax.ShapeDtypeStruct((m, d), F32),
        scratch_shapes=[pltpu.VMEM((tm, d), BF16), pltpu.VMEM((tm, d), F32)],
        compiler_params=_params(("arbitrary", "arbitrary")),
        name="mlp",
    )(x2d, g2.reshape(1, d), mod3, mod3, mod3, g_final.reshape(1, d), w1, w2)


def kernel(x, c, ctx, c_ctx, w_ada, b_ada, g_norm1, w_in, lam_q1, lam_k1, lam_q2, lam_k2, g_subln,
           w_attn_br, w_four_br, w_out, g_norm2, w_mlp_in, w_mlp_out, g_final):
    b, l, d = x.shape
    n_ctx = ctx.shape[1]
    assert w_ada.shape[0] == 1, "single-layer block"
    dw = w_attn_br.shape[1]
    fw = w_four_br.shape[1]
    n_heads = dw // HEAD_COLS
    gd = fw // N_FOURIER_GROUPS
    in_cols = w_in.shape[2]
    gate_col0 = 3 * dw + fw
    assert in_cols == gate_col0 + 2 * d

    cvec = jnp.zeros((8, d), F32).at[:b].set(c).at[b].set(c_ctx)
    mod = ada_mod(cvec, w_ada[0], b_ada[0])
    mod3 = mod[:b + 1].reshape((b + 1) * N_MOD, 1, d)

    tabs = rope_tables(l)
    tm, tn = 1024, 1024
    x2d = x.reshape(b * l, d)
    p2d = in_proj(x2d, g_norm1[0], mod3, 0, w_in[0], tabs, rows_per_mod=l, col_tile0=0,
                  n_cols=in_cols, rope_tiles=2 * dw // tn, gate_tile0=gate_col0 // tn, tm=tm, tn=tn)
    pc2d = in_proj(ctx.reshape(b * n_ctx, d), g_norm1[0], mod3, b, w_in[0], tabs,
                   rows_per_mod=b * n_ctx, col_tile0=dw // tn, n_cols=2 * dw,
                   rope_tiles=0, gate_tile0=0, tm=b * n_ctx, tn=tn)

    lam_vecs = [v[0].astype(F32).reshape(1, HEAD_DIM) for v in (lam_q1, lam_k1, lam_q2, lam_k2)]
    attn, (wa_bf, wf_bf, wo_bf, w1_bf, w2_bf) = diff_attn(
        p2d.reshape(b, l, in_cols), pc2d.reshape(b, n_ctx, 2 * dw), lam_vecs, g_subln[0],
        [w_attn_br[0], w_four_br[0], w_out[0], w_mlp_in[0], w_mlp_out[0]], n_heads=n_heads, tq=1024)

    n1 = FOURIER_N1
    n2 = l // n1
    sub = FOURIER_N1
    assert n1 * n2 == l and n2 % sub == 0
    cos_c, sin_c = (_bf16_const(t) for t in dft_tables(gd, gd ** -0.5))
    ka, kb = (_bf16_const(t) for t in stage_a_tables(l, n1, sub, l ** -0.5))
    cos2, sin2 = (_bf16_const(t) for t in dft_tables(n2, 1.0))
    yr, yi = fourier_a(p2d.reshape(b, n1, n2, in_cols), cos_c, sin_c, ka, kb,
                       col_block=3 * dw // fw, width=fw, tiles_per_step=4)
    four4 = fourier_c(yr, yi, cos2, sin2, k1_per_step=4)

    x_mid = merge(attn.reshape(b * l, dw), four4, _bf16_const(position_order_matrix(n1, sub)), p2d, x2d,
                  mod3, wa_bf, wf_bf, wo_bf, rows_per_mod=l, gate_col0=gate_col0)
    out = mlp(x_mid, g_norm2[0], mod3, g_final, w1_bf, w2_bf,
              rows_per_mod=l, tm=1024, tf=1024)
    return out.reshape(b, l, d)
```

```python
import functools
import math

import jax
import jax.numpy as jnp
import numpy as np
from jax import lax
from jax.experimental import pallas as pl
from jax.experimental.pallas import tpu as pltpu

GRID_W = 64
HEAD_DIM = 64
HEAD_COLS = 2 * HEAD_DIM
N_FOURIER_GROUPS = 4
N_MOD = 6
ROPE_BASE = 10000.0
EPS = 1e-6
LAM_INIT = 0.8 - 0.6 * math.exp(-0.3 * 0)

V7X_VMEM_LIMIT_BYTES = 58 * 1024 * 1024
BF16 = jnp.bfloat16
BF16_ROW_TILE = 16
F32 = jnp.float32


def _params(semantics, flags=None):
    return pltpu.CompilerParams(dimension_semantics=semantics,
                                vmem_limit_bytes=V7X_VMEM_LIMIT_BYTES, flags=flags)


def _dot(a, b):
    return jnp.dot(a, b, preferred_element_type=F32)


def _dot_nt(a, b):
    return lax.dot_general(a, b, (((1,), (1,)), ((), ())), preferred_element_type=F32)


def _resident(shape):
    zeros = (0,) * len(shape)
    return pl.BlockSpec(shape, lambda *_: zeros, pipeline_mode=pl.Buffered(1))


def _ada_kernel(c_ref, w_ref, b_ref, o_ref):
    cv = c_ref[...]
    o_ref[...] = _dot(cv * jax.nn.sigmoid(cv), w_ref[...]) + b_ref[...]


def ada_mod(cvec, w_ada, b_ada, *, n, tn=1024):
    rows, d = cvec.shape
    return pl.pallas_call(
        _ada_kernel,
        grid=(n // tn,),
        in_specs=[pl.BlockSpec((rows, d), lambda j: (0, 0)),
                  pl.BlockSpec((d, tn), lambda j: (0, j)),
                  pl.BlockSpec((1, tn), lambda j: (0, j))],
        out_specs=pl.BlockSpec((rows, tn), lambda j: (0, j)),
        out_shape=jax.ShapeDtypeStruct((rows, n), F32),
        compiler_params=_params(("arbitrary",)),
        name="ada_mod",
    )(cvec, w_ada, b_ada.reshape(1, -1))


def _rms(x):
    return x * lax.rsqrt(jnp.mean(x * x, axis=-1, keepdims=True) + EPS)


def _norm_mod(x, g, shift, scale):
    return _rms(x) * g * (1.0 + scale) + shift


ROW_SUB = 256


def _rope(xs, cos, sin_lo, sin_hi):
    half = HEAD_DIM // 4
    up = pltpu.roll(xs, HEAD_COLS - half, 1)
    dn = pltpu.roll(xs, half, 1)
    return xs * cos + up * sin_lo + dn * sin_hi


def _in_proj_kernel(x_ref, g_ref, sh_ref, sc_ref, w_ref, tab_ref, o_ref, h_ref, *,
                    rope_tiles, gate_tile0):
    j = pl.program_id(1)
    tm, tn = o_ref.shape
    n_sub = tm // ROW_SUB

    def run(first, kind):
        w = w_ref[...].astype(BF16)
        accs = {}

        def issue(idx):
            rows = slice(idx * ROW_SUB, (idx + 1) * ROW_SUB)
            if first:
                h = _norm_mod(x_ref[rows], g_ref[...], sh_ref[0], sc_ref[0]).astype(BF16)
                h_ref[rows] = h
            else:
                h = h_ref[rows]
            accs[idx] = _dot(h, w)

        issue(0)
        for idx in range(n_sub):
            if idx + 1 < n_sub:
                issue(idx + 1)
            acc = accs.pop(idx)
            rows = slice(idx * ROW_SUB, (idx + 1) * ROW_SUB)
            if kind == "rope":
                tab = tab_ref[0, rows]
                cos = tab[:, :HEAD_COLS]
                sin_lo = tab[:, HEAD_COLS:2 * HEAD_COLS]
                sin_hi = tab[:, 2 * HEAD_COLS:]
                for c in range(tn // HEAD_COLS):
                    cols = slice(c * HEAD_COLS, (c + 1) * HEAD_COLS)
                    o_ref[rows, cols] = _rope(acc[:, cols], cos, sin_lo, sin_hi).astype(o_ref.dtype)
            elif kind == "gate":
                o_ref[rows] = jax.nn.sigmoid(acc).astype(o_ref.dtype)
            else:
                o_ref[rows] = acc.astype(o_ref.dtype)

    if rope_tiles:
        pl.when(j == 0)(functools.partial(run, True, "rope"))
        pl.when((j > 0) & (j < rope_tiles))(functools.partial(run, False, "rope"))
        pl.when((j >= rope_tiles) & (j < gate_tile0))(functools.partial(run, False, "plain"))
        pl.when(j >= gate_tile0)(functools.partial(run, False, "gate"))
    else:
        pl.when(j == 0)(functools.partial(run, True, "plain"))
        pl.when(j > 0)(functools.partial(run, False, "plain"))


def in_proj(x2d, g, mod3, mod_row0, w, tabs, *, rows_per_mod, col_tile0, n_cols,
            rope_tiles, gate_tile0, tm, tn):
    m, d = x2d.shape
    tiles_per_mod = rows_per_mod // tm
    q_tiles = rope_tiles // 2 if rope_tiles else 1
    ltiles = tabs.shape[1] // tm if rope_tiles else 1

    def mod_map(which):
        return lambda i, j: ((mod_row0 + i // tiles_per_mod) * N_MOD + which, 0, 0)

    if rope_tiles:
        tab_spec = pl.BlockSpec((1, tm, 3 * HEAD_COLS),
                                lambda i, j: (jnp.minimum(j // q_tiles, 1), i % ltiles, 0))
    else:
        tab_spec = pl.BlockSpec((1, 8, 3 * HEAD_COLS), lambda i, j: (0, 0, 0))

    kern = functools.partial(_in_proj_kernel, rope_tiles=rope_tiles, gate_tile0=gate_tile0)
    return pl.pallas_call(
        kern,
        grid=(m // tm, n_cols // tn),
        in_specs=[pl.BlockSpec((tm, d), lambda i, j: (i, 0)),
                  pl.BlockSpec((1, d), lambda i, j: (0, 0)),
                  pl.BlockSpec((1, 1, d), mod_map(0)),
                  pl.BlockSpec((1, 1, d), mod_map(1)),
                  pl.BlockSpec((d, tn), lambda i, j: (0, j + col_tile0)),
                  tab_spec],
        out_specs=pl.BlockSpec((tm, tn), lambda i, j: (i, j)),
        out_shape=jax.ShapeDtypeStruct((m, n_cols), BF16),
        scratch_shapes=[pltpu.VMEM((tm, d), BF16)],
        compiler_params=_params(("arbitrary", "arbitrary")),
        name="in_proj",
    )(x2d, g.reshape(1, d), mod3, mod3, w, tabs)


def rope_tables(seq_len):
    pos = np.arange(seq_len)
    row = pos // GRID_W
    col = pos % GRID_W
    half = HEAD_DIM // 2
    inv_freq = ROPE_BASE ** (-np.arange(0, half, 2, dtype=np.float64) / half)

    def cs(p):
        ang = p[:, None].astype(np.float64) * inv_freq[None, :]
        ang = np.concatenate([ang, ang], axis=-1)
        return np.cos(ang), np.sin(ang)

    cr, sr = cs(row)
    cc, sc = cs(col)
    cos64 = np.concatenate([cr, cc], axis=-1)
    sin64 = np.concatenate([sr, sc], axis=-1)
    lower = (np.arange(HEAD_DIM) % half) < (half // 2)
    sin_lo64 = np.where(lower[None, :], -sin64, 0.0)
    sin_hi64 = np.where(lower[None, :], 0.0, sin64)
    tab = np.concatenate([np.tile(cos64, (1, 2)), np.tile(sin_lo64, (1, 2)),
                          np.tile(sin_hi64, (1, 2))], axis=-1)
    q_scale = HEAD_DIM ** -0.5 * math.log2(math.e)
    return jnp.asarray(np.stack([tab * q_scale, tab]), dtype=F32)


ATTN_KEY_CHUNK = 1024
ATTN_LOOKAHEAD = 2
ATTN_UNSHIFTED_LIMIT = 80.0
ATTN_UNSHIFTED_VALUE_LIMIT = 2.0 ** 30


def _component_sq_norms(x, first):
    return jnp.sum(jnp.where(first, x, 0.0), axis=-1, keepdims=True)


def _attn_kernel(lq1_ref, lk1_ref, lq2_ref, lk2_ref, gs_ref, q_ref, qh_ref, k_ref, kc_ref, v_ref,
                 vc_ref, cb_ref, wada_ref, bada_ref, *rest, n_cast):
    w_refs, o_ref, wbf_refs = rest[:n_cast], rest[n_cast], rest[n_cast + 1:-4]
    modb_ref, vt_ref, kn_ref, sb_ref = rest[-4:]
    n_lat = k_ref.shape[1]
    first_step = ((pl.program_id(0) == 0) & (pl.program_id(1) == 0) & (pl.program_id(2) == 0))

    @pl.when(first_step)
    def _():
        cb = cb_ref[...]
        sb_ref[...] = cb * jax.nn.sigmoid(cb)

    for w_ref, wbf_ref in zip(w_refs, wbf_refs):
        wbf_ref[...] = w_ref[...].astype(wbf_ref.dtype)

    @pl.when(pl.program_id(2) == 0)
    def _():
        for off in range(0, n_lat, ATTN_KEY_CHUNK):
            vt_ref[:, off:off + ATTN_KEY_CHUNK] = (
                v_ref[0, off:off + ATTN_KEY_CHUNK].astype(F32).T.astype(BF16))
        vt_ref[:, n_lat:] = vc_ref[0].astype(F32).T.astype(BF16)
        bound_sq = None
        for comp in range(2):
            tops = []
            for rows in (qh_ref[0], k_ref[0], kc_ref[0]):
                r32 = rows.astype(F32)
                first = lax.broadcasted_iota(jnp.int32, r32.shape, 1) < HEAD_DIM
                norms = _component_sq_norms(r32 * r32, first == (comp == 0))
                tops.append(jnp.max(norms, axis=0, keepdims=True))
            prod = tops[0] * jnp.maximum(tops[1], tops[2])
            bound_sq = prod if bound_sq is None else jnp.maximum(bound_sq, prod)
        kn_ref[0] = jnp.broadcast_to(bound_sq, kn_ref.shape[1:])
        v_top = jnp.maximum(jnp.max(jnp.abs(v_ref[0].astype(F32)), axis=0, keepdims=True),
                            jnp.max(jnp.abs(vc_ref[0].astype(F32)), axis=0, keepdims=True))
        kn_ref[1] = jnp.broadcast_to(jnp.max(v_top, axis=-1, keepdims=True), kn_ref.shape[1:])

    lam = (jnp.exp(jnp.sum(lq1_ref[...] * lk1_ref[...], axis=-1, keepdims=True))
           - jnp.exp(jnp.sum(lq2_ref[...] * lk2_ref[...], axis=-1, keepdims=True))
           + LAM_INIT)
    q = q_ref[0]
    lane = lax.broadcasted_iota(jnp.int32, q.shape, 1)
    qms = [jnp.where((lane < HEAD_DIM) == (comp == 0), q, jnp.zeros_like(q)) for comp in range(2)]
    chunks = ([(off, ATTN_KEY_CHUNK) for off in range(0, n_lat, ATTN_KEY_CHUNK)]
              + [(n_lat, kc_ref.shape[1])])
    items = [(off, width, comp) for off, width in chunks for comp in range(2)]
    ahead = 2 * ATTN_LOOKAHEAD

    small_scores = ((kn_ref[0, 0:1, 0:1][0, 0] <= ATTN_UNSHIFTED_LIMIT ** 2)
                    & (kn_ref[1, 0:1, 0:1][0, 0] <= ATTN_UNSHIFTED_VALUE_LIMIT))

    def run(shifted):
        w_slab = wada_ref[...]
        mods = [jnp.sum(w_slab * sb_ref[r], axis=0, keepdims=True) + bada_ref[...]
                for r in range(sb_ref.shape[0])]
        mods.append(jnp.zeros((modb_ref.shape[0] - len(mods), modb_ref.shape[1]), F32))
        modb_ref[...] = jnp.concatenate(mods, axis=0)

        scores = {}

        def issue(i):
            off, width, comp = items[i]
            keys = k_ref[0, off:off + width] if off < n_lat else kc_ref[0]
            scores[i] = _dot_nt(keys, qms[comp])

        for i in range(min(ahead, len(items))):
            issue(i)
        m = [None, None]
        d = [None, None]
        acc = [None, None]
        for i, (off, width, comp) in enumerate(items):
            if i + ahead < len(items):
                issue(i + ahead)
            s = scores.pop(i)
            vals_t = vt_ref[:, off:off + width]
            if not shifted:
                p = jnp.exp2(s)
                part = jnp.sum(p, axis=0, keepdims=True)
                pv = _dot(vals_t, p.astype(BF16))
                d[comp] = part if d[comp] is None else d[comp] + part
                acc[comp] = pv if acc[comp] is None else acc[comp] + pv
                continue
            col_max = jnp.max(s, axis=0, keepdims=True)
            if m[comp] is None:
                m[comp] = col_max
                p = jnp.exp2(s - col_max)
                d[comp] = jnp.sum(p, axis=0, keepdims=True)
                acc[comp] = _dot(vals_t, p.astype(BF16))
            else:
                m_new = jnp.maximum(m[comp], col_max)
                alpha = jnp.exp2(m[comp] - m_new)
                p = jnp.exp2(s - m_new)
                d[comp] = alpha * d[comp] + jnp.sum(p, axis=0, keepdims=True)
                acc[comp] = alpha * acc[comp] + _dot(vals_t, p.astype(BF16))
                m[comp] = m_new
        o_t = acc[0] * (1.0 / d[0]) - acc[1] * (lam / d[1])
        o = _rms(o_t.T) * gs_ref[...] * (1.0 - LAM_INIT)
        o_ref[0] = o.astype(o_ref.dtype)

    pl.when(small_scores)(functools.partial(run, False))
    pl.when(jnp.logical_not(small_scores))(functools.partial(run, True))


def diff_attn(p3, pc3, lam_vecs, g_subln, cast_weights, cvec_b, w_ada, b_ada, *, n_heads, tq):
    b, l, _ = p3.shape
    ctx = pc3.shape[1]
    h = n_heads
    nq = l // tq
    n_steps = b * h * nq
    vec = pl.BlockSpec((1, HEAD_DIM), lambda bi, hi, qi: (0, 0))

    def slab_spec(w):
        rows, cols = w.shape
        blk = max(BF16_ROW_TILE, rows // n_steps)
        last = rows // blk - 1
        return pl.BlockSpec(
            (blk, cols), lambda bi, hi, qi: (jnp.minimum((bi * h + hi) * nq + qi, last), 0))

    slabs = [slab_spec(w) for w in cast_weights]
    d_model = w_ada.shape[0]
    mod_col0 = w_ada.shape[1] // 128 - n_steps

    def mod_map(bi, hi, qi):
        return 0, mod_col0 + (bi * h + hi) * nq + qi

    outs = pl.pallas_call(
        functools.partial(_attn_kernel, n_cast=len(cast_weights)),
        grid=(b, h, nq),
        in_specs=[vec, vec, vec, vec,
                  pl.BlockSpec((1, HEAD_COLS), lambda bi, hi, qi: (0, 0)),
                  pl.BlockSpec((1, tq, HEAD_COLS), lambda bi, hi, qi: (bi, qi, hi)),
                  pl.BlockSpec((1, l, HEAD_COLS), lambda bi, hi, qi: (bi, 0, hi)),
                  pl.BlockSpec((1, l, HEAD_COLS), lambda bi, hi, qi: (bi, 0, h + hi)),
                  pl.BlockSpec((1, ctx, HEAD_COLS), lambda bi, hi, qi: (bi, 0, hi)),
                  pl.BlockSpec((1, l, HEAD_COLS), lambda bi, hi, qi: (bi, 0, 2 * h + hi)),
                  pl.BlockSpec((1, ctx, HEAD_COLS), lambda bi, hi, qi: (bi, 0, h + hi)),
                  _resident(cvec_b.shape),
                  pl.BlockSpec((d_model, 128), mod_map),
                  pl.BlockSpec((1, 128), mod_map)] + slabs,
        out_specs=[pl.BlockSpec((1, tq, HEAD_COLS), lambda bi, hi, qi: (bi, qi, hi))] + slabs
                  + [pl.BlockSpec((8, 128), lambda bi, hi, qi: (0, (bi * h + hi) * nq + qi))],
        out_shape=[jax.ShapeDtypeStruct((b, l, h * HEAD_COLS), BF16)]
                  + [jax.ShapeDtypeStruct(w.shape, BF16) for w in cast_weights]
                  + [jax.ShapeDtypeStruct((8, n_steps * 128), F32)],
        scratch_shapes=[pltpu.VMEM((HEAD_COLS, l + ctx), BF16),
                        pltpu.VMEM((2, 8, 128), F32),
                        pltpu.VMEM(cvec_b.shape, F32)],
        compiler_params=_params(("arbitrary", "arbitrary", "arbitrary")),
        name="diff_attn",
    )(*lam_vecs, g_subln.reshape(1, HEAD_COLS), p3, p3, p3, pc3, p3, pc3, cvec_b, w_ada,
      b_ada.reshape(1, -1), *cast_weights)
    return outs[0], outs[1:-1], outs[-1]


FOURIER_N1 = 16


def _bf16_const(a):
    return jnp.asarray(a, dtype=F32).astype(BF16)


def dft_tables(n, scale):
    idx = (np.arange(n)[:, None] * np.arange(n)[None, :]) % n
    ang = 2.0 * np.pi * idx / n
    return np.cos(ang) * scale, np.sin(ang) * scale


def stage_a_tables(l, n1, sub, scale):
    n2 = l // n1
    j = np.arange(n2 // sub)[:, None, None, None]
    k1 = np.arange(n1)[None, :, None, None]
    a = np.arange(sub)[None, None, :, None]
    m1 = np.arange(n1)[None, None, None, :]
    ang = 2.0 * np.pi * ((k1 * (m1 * n2 + j * sub + a)) % l) / l
    eye = np.eye(sub)
    c = np.einsum("jkam,ab->jkamb", np.cos(ang) * scale, eye).reshape(n2 // sub, n1 * sub, n1 * sub)
    s = np.einsum("jkam,ab->jkamb", np.sin(ang) * scale, eye).reshape(n2 // sub, n1 * sub, n1 * sub)
    return np.concatenate([c, -s], axis=1), np.concatenate([-s, -c], axis=1)


def _fourier_a_kernel(cc_ref, sc_ref, ka_ref, kb_ref, u_ref, yr_ref, yi_ref, *, gd, sub):
    rows = ka_ref.shape[2]
    for t in range(ka_ref.shape[0]):
        inner = slice(t * sub, (t + 1) * sub)
        u = u_ref[0, :, inner].reshape(rows, u_ref.shape[3])
        groups = range(u.shape[1] // gd)
        uc = jnp.concatenate([_dot(u[:, g * gd:(g + 1) * gd], cc_ref[...]) for g in groups], axis=1)
        us = jnp.concatenate([_dot(u[:, g * gd:(g + 1) * gd], sc_ref[...]) for g in groups], axis=1)
        y = _dot(ka_ref[t], uc.astype(BF16)) + _dot(kb_ref[t], us.astype(BF16))
        block = (yr_ref.shape[1], sub, yr_ref.shape[3])
        yr_ref[0, :, inner] = y[:rows].astype(yr_ref.dtype).reshape(block)
        yi_ref[0, :, inner] = y[rows:].astype(yi_ref.dtype).reshape(block)


def fourier_a(p4, cos_c, sin_c, ka, kb, *, col_block, width, tiles_per_step):
    b, n1, n2, _ = p4.shape
    gd = cos_c.shape[0]
    tiles, rows2, rows = ka.shape
    sub = rows // n1
    tps = tiles_per_step
    tab = pl.BlockSpec((tps, rows2, rows), lambda bi, j: (j, 0, 0))
    out_spec = pl.BlockSpec((1, n1, tps * sub, width), lambda bi, j: (bi, 0, j, 0))
    out = jax.ShapeDtypeStruct((b, n1, n2, width), BF16)
    return pl.pallas_call(
        functools.partial(_fourier_a_kernel, gd=gd, sub=sub),
        grid=(b, tiles // tps),
        in_specs=[_resident((gd, gd)), _resident((gd, gd)), tab, tab,
                  pl.BlockSpec((1, n1, tps * sub, width), lambda bi, j: (bi, 0, j, col_block))],
        out_specs=[out_spec, out_spec],
        out_shape=[out, out],
        compiler_params=_params(("arbitrary", "arbitrary")),
        name="fourier_a",
    )(cos_c, sin_c, ka, kb, p4)


def _fourier_c_kernel(c_ref, s_ref, yr_ref, yi_ref, o_ref):
    for t in range(yr_ref.shape[1]):
        o = _dot(c_ref[...], yr_ref[0, t]) + _dot(s_ref[...], yi_ref[0, t])
        o_ref[0, t] = o.astype(o_ref.dtype)


def fourier_c(yr4, yi4, cos2, sin2, *, k1_per_step):
    b, n1, n2, width = yr4.shape
    data = pl.BlockSpec((1, k1_per_step, n2, width), lambda bi, k: (bi, k, 0, 0))
    return pl.pallas_call(
        _fourier_c_kernel,
        grid=(b, n1 // k1_per_step),
        in_specs=[_resident((n2, n2)), _resident((n2, n2)), data, data],
        out_specs=data,
        out_shape=jax.ShapeDtypeStruct((b, n1, n2, width), BF16),
        compiler_params=_params(("arbitrary", "arbitrary")),
        name="fourier_c",
    )(cos2, sin2, yr4, yi4)


def position_order_matrix(n1, sub):
    p = np.zeros((sub * n1, n1 * sub))
    for k1 in range(n1):
        for k2 in range(sub):
            p[k2 * n1 + k1, k1 * sub + k2] = 1.0
    return p


def _merge_kernel(a_ref, f_ref, ga_ref, gf_ref, x_ref, gt_ref, pm_ref, wa_ref, wf_ref, wo_ref, o_ref):
    ya = _dot(a_ref[...], wa_ref[...])
    f = f_ref[0].reshape(pm_ref.shape[1], f_ref.shape[3])
    yf = _dot(_dot(pm_ref[...], f).astype(BF16), wf_ref[...])
    merged = ga_ref[...].astype(F32) * ya + gf_ref[...].astype(F32) * yf
    y = _dot(merged.astype(BF16), wo_ref[...])
    o_ref[...] = x_ref[...] + gt_ref[0] * y


def merge(attn2d, four4, pm, p2d, x2d, mod3, wa, wf, wo, *, rows_per_mod, gate_col0):
    m, d = x2d.shape
    dw = attn2d.shape[1]
    _, n1, _, fw = four4.shape
    tm = pm.shape[0]
    sub = tm // n1
    tiles_per_mod = rows_per_mod // tm
    ga_blk = gate_col0 // d
    return pl.pallas_call(
        _merge_kernel,
        grid=(m // tm,),
        in_specs=[pl.BlockSpec((tm, dw), lambda i: (i, 0)),
                  pl.BlockSpec((1, n1, sub, fw), lambda i: (i // tiles_per_mod, 0, i % tiles_per_mod, 0)),
                  pl.BlockSpec((tm, d), lambda i: (i, ga_blk)),
                  pl.BlockSpec((tm, d), lambda i: (i, ga_blk + 1)),
                  pl.BlockSpec((tm, d), lambda i: (i, 0)),
                  pl.BlockSpec((1, 1, d), lambda i: ((i // tiles_per_mod) * N_MOD + 2, 0, 0)),
                  _resident(pm.shape), _resident(wa.shape), _resident(wf.shape), _resident(wo.shape)],
        out_specs=pl.BlockSpec((tm, d), lambda i: (i, 0)),
        out_shape=jax.ShapeDtypeStruct((m, d), F32),
        compiler_params=_params(("arbitrary",)),
        name="merge",
    )(attn2d, four4, p2d, p2d, x2d, mod3, pm, wa, wf, wo)


def _mlp_kernel(x_ref, g_ref, sh_ref, sc_ref, gt_ref, gf_ref, w1_ref, w2_ref, o_ref, h_ref, acc_ref):
    j = pl.program_id(1)
    last = pl.num_programs(1) - 1
    n_sub = x_ref.shape[0] // ROW_SUB

    def run(first, final):
        zs = {}

        def issue(idx):
            rows = slice(idx * ROW_SUB, (idx + 1) * ROW_SUB)
            if first:
                h = _norm_mod(x_ref[rows], g_ref[...], sh_ref[0], sc_ref[0]).astype(BF16)
                h_ref[rows] = h
            else:
                h = h_ref[rows]
            zs[idx] = jnp.maximum(_dot(h, w1_ref[...]), 0.0)

        issue(0)
        for idx in range(n_sub):
            if idx + 1 < n_sub:
                issue(idx + 1)
            z = zs.pop(idx)
            rows = slice(idx * ROW_SUB, (idx + 1) * ROW_SUB)
            total = _dot((z * z).astype(BF16), w2_ref[...])
            if not first:
                total = acc_ref[rows] + total
            if final:
                y = x_ref[rows] + gt_ref[0] * total
                o_ref[rows] = _rms(y) * gf_ref[...]
            else:
                acc_ref[rows] = total

    pl.when(j == 0)(functools.partial(run, True, False))
    pl.when((j > 0) & (j < last))(functools.partial(run, False, False))
    pl.when(j == last)(functools.partial(run, False, True))


def mlp(x2d, g2, mod3, g_final, w1, w2, *, rows_per_mod, tm, tf):
    m, d = x2d.shape
    f = w1.shape[1]
    assert f // tf >= 2 and tm % ROW_SUB == 0
    tiles_per_mod = rows_per_mod // tm

    def mod_map(which):
        return lambda i, j: ((i // tiles_per_mod) * N_MOD + which, 0, 0)

    return pl.pallas_call(
        _mlp_kernel,
        grid=(m // tm, f // tf),
        in_specs=[pl.BlockSpec((tm, d), lambda i, j: (i, 0)),
                  pl.BlockSpec((1, d), lambda i, j: (0, 0)),
                  pl.BlockSpec((1, 1, d), mod_map(3)),
                  pl.BlockSpec((1, 1, d), mod_map(4)),
                  pl.BlockSpec((1, 1, d), mod_map(5)),
                  pl.BlockSpec((1, d), lambda i, j: (0, 0)),
                  pl.BlockSpec((d, tf), lambda i, j: (0, j)),
                  pl.BlockSpec((tf, d), lambda i, j: (j, 0))],
        out_specs=pl.BlockSpec((tm, d), lambda i, j: (i, 0), pipeline_mode=pl.Buffered(1)),
        out_shape=jax.ShapeDtypeStruct((m, d), F32),
        scratch_shapes=[pltpu.VMEM((tm, d), BF16), pltpu.VMEM((tm, d), F32)],
        compiler_params=_params(("arbitrary", "arbitrary")),
        name="mlp",
    )(x2d, g2.reshape(1, d), mod3, mod3, mod3, g_final.reshape(1, d), w1, w2)


def kernel(x, c, ctx, c_ctx, w_ada, b_ada, g_norm1, w_in, lam_q1, lam_k1, lam_q2, lam_k2, g_subln,
           w_attn_br, w_four_br, w_out, g_norm2, w_mlp_in, w_mlp_out, g_final):
    b, l, d = x.shape
    n_ctx = ctx.shape[1]
    assert w_ada.shape[0] == 1, "single-layer block"
    dw = w_attn_br.shape[1]
    fw = w_four_br.shape[1]
    n_heads = dw // HEAD_COLS
    gd = fw // N_FOURIER_GROUPS
    in_cols = w_in.shape[2]
    gate_col0 = 3 * dw + fw
    assert in_cols == gate_col0 + 2 * d

    cvec = jnp.zeros((8, d), F32).at[:b].set(c).at[b].set(c_ctx)
    n_early = 2 * d
    mod_a = ada_mod(cvec, w_ada[0], b_ada[0], n=n_early)

    def mod_rows(mod):
        return mod[:b + 1].reshape((b + 1) * N_MOD, 1, d)

    mod3_early = mod_rows(jnp.pad(mod_a, ((0, 0), (0, N_MOD * d - n_early))))

    tabs = rope_tables(l)
    tm, tn = 1024, 1024
    x2d = x.reshape(b * l, d)
    p2d = in_proj(x2d, g_norm1[0], mod3_early, 0, w_in[0], tabs, rows_per_mod=l, col_tile0=0,
                  n_cols=in_cols, rope_tiles=2 * dw // tn, gate_tile0=gate_col0 // tn, tm=tm, tn=tn)
    pc2d = in_proj(ctx.reshape(b * n_ctx, d), g_norm1[0], mod3_early, b, w_in[0], tabs,
                   rows_per_mod=b * n_ctx, col_tile0=dw // tn, n_cols=2 * dw,
                   rope_tiles=0, gate_tile0=0, tm=b * n_ctx, tn=tn)

    lam_vecs = [v[0].astype(F32).reshape(1, HEAD_DIM) for v in (lam_q1, lam_k1, lam_q2, lam_k2)]
    tq = 1024
    assert (N_MOD * d - n_early) == (b * n_heads * (l // tq)) * 128
    cvec_b = jnp.broadcast_to(cvec[:b + 1, :, None], (b + 1, d, 128))
    attn, (wa_bf, wf_bf, wo_bf, w1_bf, w2_bf), mod_b = diff_attn(
        p2d.reshape(b, l, in_cols), pc2d.reshape(b, n_ctx, 2 * dw), lam_vecs, g_subln[0],
        [w_attn_br[0], w_four_br[0], w_out[0], w_mlp_in[0], w_mlp_out[0]],
        cvec_b, w_ada[0], b_ada[0], n_heads=n_heads, tq=tq)
    mod3 = mod_rows(jnp.concatenate([mod_a, mod_b], axis=1))

    n1 = FOURIER_N1
    n2 = l // n1
    sub = FOURIER_N1
    assert n1 * n2 == l and n2 % sub == 0
    cos_c, sin_c = (_bf16_const(t) for t in dft_tables(gd, gd ** -0.5))
    ka, kb = (_bf16_const(t) for t in stage_a_tables(l, n1, sub, l ** -0.5))
    cos2, sin2 = (_bf16_const(t) for t in dft_tables(n2, 1.0))
    yr, yi = fourier_a(p2d.reshape(b, n1, n2, in_cols), cos_c, sin_c, ka, kb,
                       col_block=3 * dw // fw, width=fw, tiles_per_step=4)
    four4 = fourier_c(yr, yi, cos2, sin2, k1_per_step=4)

    x_mid = merge(attn.reshape(b * l, dw), four4, _bf16_const(position_order_matrix(n1, sub)), p2d, x2d,
                  mod3, wa_bf, wf_bf, wo_bf, rows_per_mod=l, gate_col0=gate_col0)
    out = mlp(x_mid, g_norm2[0], mod3, g_final, w1_bf, w2_bf,
              rows_per_mod=l, tm=1024, tf=1024)
    return out.reshape(b, l, d)
```

```python
import functools
import math

import jax
import jax.numpy as jnp
import numpy as np
from jax import lax
from jax.experimental import pallas as pl
from jax.experimental.pallas import tpu as pltpu

GRID_W = 64
HEAD_DIM = 64
HEAD_COLS = 2 * HEAD_DIM
N_FOURIER_GROUPS = 4
N_MOD = 6
ROPE_BASE = 10000.0
EPS = 1e-6
LAM_INIT = 0.8 - 0.6 * math.exp(-0.3 * 0)

V7X_VMEM_LIMIT_BYTES = 58 * 1024 * 1024
BF16 = jnp.bfloat16
BF16_ROW_TILE = 16
F32 = jnp.float32


def _params(semantics):
    return pltpu.CompilerParams(dimension_semantics=semantics,
                                vmem_limit_bytes=V7X_VMEM_LIMIT_BYTES)


def _dot(a, b):
    return jnp.dot(a, b, preferred_element_type=F32)


def _dot_nt(a, b):
    return lax.dot_general(a, b, (((1,), (1,)), ((), ())), preferred_element_type=F32)


def _resident(shape):
    zeros = (0,) * len(shape)
    return pl.BlockSpec(shape, lambda *_: zeros, pipeline_mode=pl.Buffered(1))


def _ada_kernel(c_ref, w_ref, b_ref, o_ref):
    cv = c_ref[...]
    o_ref[...] = _dot(cv * jax.nn.sigmoid(cv), w_ref[...]) + b_ref[...]


def ada_mod(cvec, w_ada, b_ada, *, tn=1024):
    rows, d = cvec.shape
    n = w_ada.shape[1]
    return pl.pallas_call(
        _ada_kernel,
        grid=(n // tn,),
        in_specs=[pl.BlockSpec((rows, d), lambda j: (0, 0)),
                  pl.BlockSpec((d, tn), lambda j: (0, j)),
                  pl.BlockSpec((1, tn), lambda j: (0, j))],
        out_specs=pl.BlockSpec((rows, tn), lambda j: (0, j)),
        out_shape=jax.ShapeDtypeStruct((rows, n), F32),
        compiler_params=_params(("arbitrary",)),
        name="ada_mod",
    )(cvec, w_ada, b_ada.reshape(1, n))


def _rms(x):
    return x * lax.rsqrt(jnp.mean(x * x, axis=-1, keepdims=True) + EPS)


def _norm_mod(x, g, shift, scale):
    return _rms(x) * g * (1.0 + scale) + shift


ROW_SUB = 256


def _rope(xs, cos, sin_lo, sin_hi):
    half = HEAD_DIM // 4
    up = pltpu.roll(xs, HEAD_COLS - half, 1)
    dn = pltpu.roll(xs, half, 1)
    return xs * cos + up * sin_lo + dn * sin_hi


def _in_proj_kernel(x_ref, g_ref, sh_ref, sc_ref, w_ref, tab_ref, o_ref, h_ref, *,
                    rope_tiles, gate_tile0):
    j = pl.program_id(1)
    tm, tn = o_ref.shape
    n_sub = tm // ROW_SUB

    def run(first, kind):
        w = w_ref[...].astype(BF16)
        accs = {}

        def issue(idx):
            rows = slice(idx * ROW_SUB, (idx + 1) * ROW_SUB)
            if first:
                h = _norm_mod(x_ref[rows], g_ref[...], sh_ref[0], sc_ref[0]).astype(BF16)
                h_ref[rows] = h
            else:
                h = h_ref[rows]
            accs[idx] = _dot(h, w)

        issue(0)
        for idx in range(n_sub):
            if idx + 1 < n_sub:
                issue(idx + 1)
            acc = accs.pop(idx)
            rows = slice(idx * ROW_SUB, (idx + 1) * ROW_SUB)
            if kind == "rope":
                tab = tab_ref[0, rows]
                cos = tab[:, :HEAD_COLS]
                sin_lo = tab[:, HEAD_COLS:2 * HEAD_COLS]
                sin_hi = tab[:, 2 * HEAD_COLS:]
                for c in range(tn // HEAD_COLS):
                    cols = slice(c * HEAD_COLS, (c + 1) * HEAD_COLS)
                    o_ref[rows, cols] = _rope(acc[:, cols], cos, sin_lo, sin_hi).astype(o_ref.dtype)
            elif kind == "gate":
                o_ref[rows] = jax.nn.sigmoid(acc).astype(o_ref.dtype)
            else:
                o_ref[rows] = acc.astype(o_ref.dtype)

    if rope_tiles:
        pl.when(j == 0)(functools.partial(run, True, "rope"))
        pl.when((j > 0) & (j < rope_tiles))(functools.partial(run, False, "rope"))
        pl.when((j >= rope_tiles) & (j < gate_tile0))(functools.partial(run, False, "plain"))
        pl.when(j >= gate_tile0)(functools.partial(run, False, "gate"))
    else:
        pl.when(j == 0)(functools.partial(run, True, "plain"))
        pl.when(j > 0)(functools.partial(run, False, "plain"))


def in_proj(x2d, g, mod3, mod_row0, w, tabs, *, rows_per_mod, col_tile0, n_cols,
            rope_tiles, gate_tile0, tm, tn):
    m, d = x2d.shape
    tiles_per_mod = rows_per_mod // tm
    q_tiles = rope_tiles // 2 if rope_tiles else 1
    ltiles = tabs.shape[1] // tm if rope_tiles else 1

    def mod_map(which):
        return lambda i, j: ((mod_row0 + i // tiles_per_mod) * N_MOD + which, 0, 0)

    if rope_tiles:
        tab_spec = pl.BlockSpec((1, tm, 3 * HEAD_COLS),
                                lambda i, j: (jnp.minimum(j // q_tiles, 1), i % ltiles, 0))
    else:
        tab_spec = pl.BlockSpec((1, 8, 3 * HEAD_COLS), lambda i, j: (0, 0, 0))

    kern = functools.partial(_in_proj_kernel, rope_tiles=rope_tiles, gate_tile0=gate_tile0)
    return pl.pallas_call(
        kern,
        grid=(m // tm, n_cols // tn),
        in_specs=[pl.BlockSpec((tm, d), lambda i, j: (i, 0)),
                  pl.BlockSpec((1, d), lambda i, j: (0, 0)),
                  pl.BlockSpec((1, 1, d), mod_map(0)),
                  pl.BlockSpec((1, 1, d), mod_map(1)),
                  pl.BlockSpec((d, tn), lambda i, j: (0, j + col_tile0)),
                  tab_spec],
        out_specs=pl.BlockSpec((tm, tn), lambda i, j: (i, j)),
        out_shape=jax.ShapeDtypeStruct((m, n_cols), BF16),
        scratch_shapes=[pltpu.VMEM((tm, d), BF16)],
        compiler_params=_params(("arbitrary", "arbitrary")),
        name="in_proj",
    )(x2d, g.reshape(1, d), mod3, mod3, w, tabs)


def rope_tables(seq_len):
    pos = np.arange(seq_len)
    row = pos // GRID_W
    col = pos % GRID_W
    half = HEAD_DIM // 2
    inv_freq = ROPE_BASE ** (-np.arange(0, half, 2, dtype=np.float64) / half)

    def cs(p):
        ang = p[:, None].astype(np.float64) * inv_freq[None, :]
        ang = np.concatenate([ang, ang], axis=-1)
        return np.cos(ang), np.sin(ang)

    cr, sr = cs(row)
    cc, sc = cs(col)
    cos64 = np.concatenate([cr, cc], axis=-1)
    sin64 = np.concatenate([sr, sc], axis=-1)
    lower = (np.arange(HEAD_DIM) % half) < (half // 2)
    sin_lo64 = np.where(lower[None, :], -sin64, 0.0)
    sin_hi64 = np.where(lower[None, :], 0.0, sin64)
    tab = np.concatenate([np.tile(cos64, (1, 2)), np.tile(sin_lo64, (1, 2)),
                          np.tile(sin_hi64, (1, 2))], axis=-1)
    q_scale = HEAD_DIM ** -0.5 * math.log2(math.e)
    return jnp.asarray(np.stack([tab * q_scale, tab]), dtype=F32)


ATTN_KEY_CHUNK = 1024
ATTN_LOOKAHEAD = 2
ATTN_UNSHIFTED_LIMIT = 80.0
ATTN_UNSHIFTED_VALUE_LIMIT = 2.0 ** 30
ATTN_NORM_MARGIN = 1.02


def _attn_kernel(lq1_ref, lk1_ref, lq2_ref, lk2_ref, gs_ref, q_ref, qh_ref, k_ref, kc_ref, v_ref,
                 vc_ref, *rest, n_cast):
    w_refs, o_ref, wbf_refs = rest[:n_cast], rest[n_cast], rest[n_cast + 1:-2]
    vt_ref, kn_ref = rest[-2:]
    n_lat = k_ref.shape[1]

    @pl.when(pl.program_id(2) == 0)
    def _():
        for off in range(0, n_lat, ATTN_KEY_CHUNK):
            vt_ref[:, off:off + ATTN_KEY_CHUNK] = (
                v_ref[0, off:off + ATTN_KEY_CHUNK].astype(F32).T.astype(BF16))
        vt_ref[:, n_lat:] = vc_ref[0].astype(F32).T.astype(BF16)
        sel = (lax.broadcasted_iota(jnp.int32, (HEAD_COLS, HEAD_COLS), 0) // HEAD_DIM
               == lax.broadcasted_iota(jnp.int32, (HEAD_COLS, HEAD_COLS), 1)).astype(BF16)
        tops = []
        for rows in (qh_ref[0], k_ref[0], kc_ref[0]):
            r32 = rows.astype(F32)
            norms = _dot((r32 * r32).astype(BF16), sel)
            tops.append(jnp.max(norms, axis=0, keepdims=True))
        prod = tops[0] * jnp.maximum(tops[1], tops[2]) * ATTN_NORM_MARGIN
        bound_sq = jnp.maximum(prod[:, 0:1], prod[:, 1:2])
        kn_ref[0] = jnp.broadcast_to(bound_sq, kn_ref.shape[1:])
        v_top = jnp.maximum(jnp.max(jnp.abs(v_ref[0].astype(F32)), axis=0, keepdims=True),
                            jnp.max(jnp.abs(vc_ref[0].astype(F32)), axis=0, keepdims=True))
        kn_ref[1] = jnp.broadcast_to(jnp.max(v_top, axis=-1, keepdims=True), kn_ref.shape[1:])

    lam = (jnp.exp(jnp.sum(lq1_ref[...] * lk1_ref[...], axis=-1, keepdims=True))
           - jnp.exp(jnp.sum(lq2_ref[...] * lk2_ref[...], axis=-1, keepdims=True))
           + LAM_INIT)
    q = q_ref[0]
    lane = lax.broadcasted_iota(jnp.int32, q.shape, 1)
    qms = [jnp.where((lane < HEAD_DIM) == (comp == 0), q, jnp.zeros_like(q)) for comp in range(2)]
    chunks = ([(off, ATTN_KEY_CHUNK) for off in range(0, n_lat, ATTN_KEY_CHUNK)]
              + [(n_lat, kc_ref.shape[1])])
    items = [(off, width, comp) for off, width in chunks for comp in range(2)]
    ahead = 2 * ATTN_LOOKAHEAD

    small_scores = ((kn_ref[0, 0:1, 0:1][0, 0] <= ATTN_UNSHIFTED_LIMIT ** 2)
                    & (kn_ref[1, 0:1, 0:1][0, 0] <= ATTN_UNSHIFTED_VALUE_LIMIT))

    def run(shifted):
        for w_ref, wbf_ref in zip(w_refs, wbf_refs):
            wbf_ref[...] = w_ref[...].astype(wbf_ref.dtype)

        scores = {}

        def issue(i):
            off, width, comp = items[i]
            keys = k_ref[0, off:off + width] if off < n_lat else kc_ref[0]
            scores[i] = _dot_nt(keys, qms[comp])

        for i in range(min(ahead, len(items))):
            issue(i)
        m = [None, None]
        d = [None, None]
        acc = [None, None]
        for i, (off, width, comp) in enumerate(items):
            if i + ahead < len(items):
                issue(i + ahead)
            s = scores.pop(i)
            vals_t = vt_ref[:, off:off + width]
            if not shifted:
                p = jnp.exp2(s)
                part = jnp.sum(p, axis=0, keepdims=True)
                pv = _dot(vals_t, p.astype(BF16))
                d[comp] = part if d[comp] is None else d[comp] + part
                acc[comp] = pv if acc[comp] is None else acc[comp] + pv
                continue
            col_max = jnp.max(s, axis=0, keepdims=True)
            if m[comp] is None:
                m[comp] = col_max
                p = jnp.exp2(s - col_max)
                d[comp] = jnp.sum(p, axis=0, keepdims=True)
                acc[comp] = _dot(vals_t, p.astype(BF16))
            else:
                m_new = jnp.maximum(m[comp], col_max)
                alpha = jnp.exp2(m[comp] - m_new)
                p = jnp.exp2(s - m_new)
                d[comp] = alpha * d[comp] + jnp.sum(p, axis=0, keepdims=True)
                acc[comp] = alpha * acc[comp] + _dot(vals_t, p.astype(BF16))
                m[comp] = m_new
        o_t = acc[0] * (1.0 / d[0]) - acc[1] * (lam / d[1])
        o = _rms(o_t.T) * gs_ref[...] * (1.0 - LAM_INIT)
        o_ref[0] = o.astype(o_ref.dtype)

    pl.when(small_scores)(functools.partial(run, False))
    pl.when(jnp.logical_not(small_scores))(functools.partial(run, True))


def diff_attn(p3, pc3, lam_vecs, g_subln, cast_weights, *, n_heads, tq):
    b, l, _ = p3.shape
    ctx = pc3.shape[1]
    h = n_heads
    nq = l // tq
    n_steps = b * h * nq
    vec = pl.BlockSpec((1, HEAD_DIM), lambda bi, hi, qi: (0, 0))

    def slab_spec(w):
        rows, cols = w.shape
        blk = max(BF16_ROW_TILE, rows // n_steps)
        last = rows // blk - 1
        return pl.BlockSpec(
            (blk, cols), lambda bi, hi, qi: (jnp.minimum((bi * h + hi) * nq + qi, last), 0))

    slabs = [slab_spec(w) for w in cast_weights]
    outs = pl.pallas_call(
        functools.partial(_attn_kernel, n_cast=len(cast_weights)),
        grid=(b, h, nq),
        in_specs=[vec, vec, vec, vec,
                  pl.BlockSpec((1, HEAD_COLS), lambda bi, hi, qi: (0, 0)),
                  pl.BlockSpec((1, tq, HEAD_COLS), lambda bi, hi, qi: (bi, qi, hi)),
                  pl.BlockSpec((1, l, HEAD_COLS), lambda bi, hi, qi: (bi, 0, hi)),
                  pl.BlockSpec((1, l, HEAD_COLS), lambda bi, hi, qi: (bi, 0, h + hi)),
                  pl.BlockSpec((1, ctx, HEAD_COLS), lambda bi, hi, qi: (bi, 0, hi)),
                  pl.BlockSpec((1, l, HEAD_COLS), lambda bi, hi, qi: (bi, 0, 2 * h + hi)),
                  pl.BlockSpec((1, ctx, HEAD_COLS), lambda bi, hi, qi: (bi, 0, h + hi))] + slabs,
        out_specs=[pl.BlockSpec((1, tq, HEAD_COLS), lambda bi, hi, qi: (bi, qi, hi))] + slabs,
        out_shape=[jax.ShapeDtypeStruct((b, l, h * HEAD_COLS), BF16)]
                  + [jax.ShapeDtypeStruct(w.shape, BF16) for w in cast_weights],
        scratch_shapes=[pltpu.VMEM((HEAD_COLS, l + ctx), BF16),
                        pltpu.VMEM((2, 8, 128), F32)],
        compiler_params=_params(("arbitrary", "arbitrary", "arbitrary")),
        name="diff_attn",
    )(*lam_vecs, g_subln.reshape(1, HEAD_COLS), p3, p3, p3, pc3, p3, pc3, *cast_weights)
    return outs[0], outs[1:]


FOURIER_N1 = 16


def _bf16_const(a):
    return jnp.asarray(a, dtype=F32).astype(BF16)


def dft_tables(n, scale):
    idx = (np.arange(n)[:, None] * np.arange(n)[None, :]) % n
    ang = 2.0 * np.pi * idx / n
    return np.cos(ang) * scale, np.sin(ang) * scale


def stage_a_tables(l, n1, sub, scale):
    n2 = l // n1
    j = np.arange(n2 // sub)[:, None, None, None]
    k1 = np.arange(n1)[None, :, None, None]
    a = np.arange(sub)[None, None, :, None]
    m1 = np.arange(n1)[None, None, None, :]
    ang = 2.0 * np.pi * ((k1 * (m1 * n2 + j * sub + a)) % l) / l
    eye = np.eye(sub)
    c = np.einsum("jkam,ab->jkamb", np.cos(ang) * scale, eye).reshape(n2 // sub, n1 * sub, n1 * sub)
    s = np.einsum("jkam,ab->jkamb", np.sin(ang) * scale, eye).reshape(n2 // sub, n1 * sub, n1 * sub)
    return np.concatenate([c, -s], axis=1), np.concatenate([-s, -c], axis=1)


def _fourier_a_kernel(cc_ref, sc_ref, ka_ref, kb_ref, u_ref, yr_ref, yi_ref, *, gd, sub):
    rows = ka_ref.shape[2]
    for t in range(ka_ref.shape[0]):
        inner = slice(t * sub, (t + 1) * sub)
        u = u_ref[0, :, inner].reshape(rows, u_ref.shape[3])
        groups = range(u.shape[1] // gd)
        uc = jnp.concatenate([_dot(u[:, g * gd:(g + 1) * gd], cc_ref[...]) for g in groups], axis=1)
        us = jnp.concatenate([_dot(u[:, g * gd:(g + 1) * gd], sc_ref[...]) for g in groups], axis=1)
        y = _dot(ka_ref[t], uc.astype(BF16)) + _dot(kb_ref[t], us.astype(BF16))
        block = (yr_ref.shape[1], sub, yr_ref.shape[3])
        yr_ref[0, :, inner] = y[:rows].astype(yr_ref.dtype).reshape(block)
        yi_ref[0, :, inner] = y[rows:].astype(yi_ref.dtype).reshape(block)


def fourier_a(p4, cos_c, sin_c, ka, kb, *, col_block, width, tiles_per_step):
    b, n1, n2, _ = p4.shape
    gd = cos_c.shape[0]
    tiles, rows2, rows = ka.shape
    sub = rows // n1
    tps = tiles_per_step
    tab = pl.BlockSpec((tps, rows2, rows), lambda bi, j: (j, 0, 0))
    out_spec = pl.BlockSpec((1, n1, tps * sub, width), lambda bi, j: (bi, 0, j, 0))
    out = jax.ShapeDtypeStruct((b, n1, n2, width), BF16)
    return pl.pallas_call(
        functools.partial(_fourier_a_kernel, gd=gd, sub=sub),
        grid=(b, tiles // tps),
        in_specs=[_resident((gd, gd)), _resident((gd, gd)), tab, tab,
                  pl.BlockSpec((1, n1, tps * sub, width), lambda bi, j: (bi, 0, j, col_block))],
        out_specs=[out_spec, out_spec],
        out_shape=[out, out],
        compiler_params=_params(("arbitrary", "arbitrary")),
        name="fourier_a",
    )(cos_c, sin_c, ka, kb, p4)


def _fourier_c_kernel(c_ref, s_ref, yr_ref, yi_ref, o_ref):
    for t in range(yr_ref.shape[1]):
        o = _dot(c_ref[...], yr_ref[0, t]) + _dot(s_ref[...], yi_ref[0, t])
        o_ref[0, t] = o.astype(o_ref.dtype)


def fourier_c(yr4, yi4, cos2, sin2, *, k1_per_step):
    b, n1, n2, width = yr4.shape
    data = pl.BlockSpec((1, k1_per_step, n2, width), lambda bi, k: (bi, k, 0, 0))
    return pl.pallas_call(
        _fourier_c_kernel,
        grid=(b, n1 // k1_per_step),
        in_specs=[_resident((n2, n2)), _resident((n2, n2)), data, data],
        out_specs=data,
        out_shape=jax.ShapeDtypeStruct((b, n1, n2, width), BF16),
        compiler_params=_params(("arbitrary", "arbitrary")),
        name="fourier_c",
    )(cos2, sin2, yr4, yi4)


def position_order_matrix(n1, sub):
    p = np.zeros((sub * n1, n1 * sub))
    for k1 in range(n1):
        for k2 in range(sub):
            p[k2 * n1 + k1, k1 * sub + k2] = 1.0
    return p


def _merge_kernel(a_ref, f_ref, ga_ref, gf_ref, x_ref, gt_ref, pm_ref, wa_ref, wf_ref, wo_ref, o_ref):
    ya = _dot(a_ref[...], wa_ref[...])
    f = f_ref[0].reshape(pm_ref.shape[1], f_ref.shape[3])
    yf = _dot(_dot(pm_ref[...], f).astype(BF16), wf_ref[...])
    merged = ga_ref[...].astype(F32) * ya + gf_ref[...].astype(F32) * yf
    y = _dot(merged.astype(BF16), wo_ref[...])
    o_ref[...] = x_ref[...] + gt_ref[0] * y


def merge(attn2d, four4, pm, p2d, x2d, mod3, wa, wf, wo, *, rows_per_mod, gate_col0):
    m, d = x2d.shape
    dw = attn2d.shape[1]
    _, n1, _, fw = four4.shape
    tm = pm.shape[0]
    sub = tm // n1
    tiles_per_mod = rows_per_mod // tm
    ga_blk = gate_col0 // d
    return pl.pallas_call(
        _merge_kernel,
        grid=(m // tm,),
        in_specs=[pl.BlockSpec((tm, dw), lambda i: (i, 0)),
                  pl.BlockSpec((1, n1, sub, fw), lambda i: (i // tiles_per_mod, 0, i % tiles_per_mod, 0)),
                  pl.BlockSpec((tm, d), lambda i: (i, ga_blk)),
                  pl.BlockSpec((tm, d), lambda i: (i, ga_blk + 1)),
                  pl.BlockSpec((tm, d), lambda i: (i, 0)),
                  pl.BlockSpec((1, 1, d), lambda i: ((i // tiles_per_mod) * N_MOD + 2, 0, 0)),
                  _resident(pm.shape), _resident(wa.shape), _resident(wf.shape), _resident(wo.shape)],
        out_specs=pl.BlockSpec((tm, d), lambda i: (i, 0)),
        out_shape=jax.ShapeDtypeStruct((m, d), F32),
        compiler_params=_params(("arbitrary",)),
        name="merge",
    )(attn2d, four4, p2d, p2d, x2d, mod3, pm, wa, wf, wo)


def _mlp_kernel(x_ref, g_ref, sh_ref, sc_ref, gt_ref, gf_ref, w1_ref, w2_ref, o_ref, h_ref, acc_ref):
    j = pl.program_id(1)
    last = pl.num_programs(1) - 1
    n_sub = x_ref.shape[0] // ROW_SUB

    def run(first, final):
        zs = {}

        def issue(idx):
            rows = slice(idx * ROW_SUB, (idx + 1) * ROW_SUB)
            if first:
                h = _norm_mod(x_ref[rows], g_ref[...], sh_ref[0], sc_ref[0]).astype(BF16)
                h_ref[rows] = h
            else:
                h = h_ref[rows]
            zs[idx] = jnp.maximum(_dot(h, w1_ref[...]), 0.0)

        issue(0)
        for idx in range(n_sub):
            if idx + 1 < n_sub:
                issue(idx + 1)
            z = zs.pop(idx)
            rows = slice(idx * ROW_SUB, (idx + 1) * ROW_SUB)
            total = _dot((z * z).astype(BF16), w2_ref[...])
            if not first:
                total = acc_ref[rows] + total
            if final:
                y = x_ref[rows] + gt_ref[0] * total
                o_ref[rows] = _rms(y) * gf_ref[...]
            else:
                acc_ref[rows] = total

    pl.when(j == 0)(functools.partial(run, True, False))
    pl.when((j > 0) & (j < last))(functools.partial(run, False, False))
    pl.when(j == last)(functools.partial(run, False, True))


def mlp(x2d, g2, mod3, g_final, w1, w2, *, rows_per_mod, tm, tf):
    m, d = x2d.shape
    f = w1.shape[1]
    assert f // tf >= 2 and tm % ROW_SUB == 0
    tiles_per_mod = rows_per_mod // tm

    def mod_map(which):
        return lambda i, j: ((i // tiles_per_mod) * N_MOD + which, 0, 0)

    return pl.pallas_call(
        _mlp_kernel,
        grid=(m // tm, f // tf),
        in_specs=[pl.BlockSpec((tm, d), lambda i, j: (i, 0)),
                  pl.BlockSpec((1, d), lambda i, j: (0, 0)),
                  pl.BlockSpec((1, 1, d), mod_map(3)),
                  pl.BlockSpec((1, 1, d), mod_map(4)),
                  pl.BlockSpec((1, 1, d), mod_map(5)),
                  pl.BlockSpec((1, d), lambda i, j: (0, 0)),
                  pl.BlockSpec((d, tf), lambda i, j: (0, j)),
                  pl.BlockSpec((tf, d), lambda i, j: (j, 0))],
        out_specs=pl.BlockSpec((tm, d), lambda i, j: (i, 0), pipeline_mode=pl.Buffered(1)),
        out_shape=jax.ShapeDtypeStruct((m, d), F32),
        scratch_shapes=[pltpu.VMEM((tm, d), BF16), pltpu.VMEM((tm, d), F32)],
        compiler_params=_params(("arbitrary", "arbitrary")),
        name="mlp",
    )(x2d, g2.reshape(1, d), mod3, mod3, mod3, g_final.reshape(1, d), w1, w2)


def kernel(x, c, ctx, c_ctx, w_ada, b_ada, g_norm1, w_in, lam_q1, lam_k1, lam_q2, lam_k2, g_subln,
           w_attn_br, w_four_br, w_out, g_norm2, w_mlp_in, w_mlp_out, g_final):
    b, l, d = x.shape
    n_ctx = ctx.shape[1]
    assert w_ada.shape[0] == 1, "single-layer block"
    dw = w_attn_br.shape[1]
    fw = w_four_br.shape[1]
    n_heads = dw // HEAD_COLS
    gd = fw // N_FOURIER_GROUPS
    in_cols = w_in.shape[2]
    gate_col0 = 3 * dw + fw
    assert in_cols == gate_col0 + 2 * d

    cvec = jnp.zeros((8, d), F32).at[:b].set(c).at[b].set(c_ctx)
    mod = ada_mod(cvec, w_ada[0], b_ada[0])
    mod3 = mod[:b + 1].reshape((b + 1) * N_MOD, 1, d)

    tabs = rope_tables(l)
    tm, tn = 1024, 1024
    x2d = x.reshape(b * l, d)
    p2d = in_proj(x2d, g_norm1[0], mod3, 0, w_in[0], tabs, rows_per_mod=l, col_tile0=0,
                  n_cols=in_cols, rope_tiles=2 * dw // tn, gate_tile0=gate_col0 // tn, tm=tm, tn=tn)
    pc2d = in_proj(ctx.reshape(b * n_ctx, d), g_norm1[0], mod3, b, w_in[0], tabs,
                   rows_per_mod=b * n_ctx, col_tile0=dw // tn, n_cols=2 * dw,
                   rope_tiles=0, gate_tile0=0, tm=b * n_ctx, tn=tn)

    lam_vecs = [v[0].astype(F32).reshape(1, HEAD_DIM) for v in (lam_q1, lam_k1, lam_q2, lam_k2)]
    attn, (wa_bf, wf_bf, wo_bf, w1_bf, w2_bf) = diff_attn(
        p2d.reshape(b, l, in_cols), pc2d.reshape(b, n_ctx, 2 * dw), lam_vecs, g_subln[0],
        [w_attn_br[0], w_four_br[0], w_out[0], w_mlp_in[0], w_mlp_out[0]], n_heads=n_heads, tq=1024)

    n1 = FOURIER_N1
    n2 = l // n1
    sub = FOURIER_N1
    assert n1 * n2 == l and n2 % sub == 0
    cos_c, sin_c = (_bf16_const(t) for t in dft_tables(gd, gd ** -0.5))
    ka, kb = (_bf16_const(t) for t in stage_a_tables(l, n1, sub, l ** -0.5))
    cos2, sin2 = (_bf16_const(t) for t in dft_tables(n2, 1.0))
    yr, yi = fourier_a(p2d.reshape(b, n1, n2, in_cols), cos_c, sin_c, ka, kb,
                       col_block=3 * dw // fw, width=fw, tiles_per_step=4)
    four4 = fourier_c(yr, yi, cos2, sin2, k1_per_step=4)

    x_mid = merge(attn.reshape(b * l, dw), four4, _bf16_const(position_order_matrix(n1, sub)), p2d, x2d,
                  mod3, wa_bf, wf_bf, wo_bf, rows_per_mod=l, gate_col0=gate_col0)
    out = mlp(x_mid, g_norm2[0], mod3, g_final, w1_bf, w2_bf,
              rows_per_mod=l, tm=1024, tf=1024)
    return out.reshape(b, l, d)
```
